```python
import jax, jax.numpy as jnp
from jax import lax
import numpy as np

D_MODEL = 1024
BATCH = 8
SEQ = 4096
DEPTH = 2

CHUNK = 64
Q_BLOCK = 128
D_FF = 2816
C_CONV = 512
CONV_WIDTH = 31
N_HEADS = 8
NOPE_DIM = 64
ROPE_DIM = 32
QK_DIM = NOPE_DIM + ROPE_DIM
V_DIM = 64
Q_LORA = 256
KV_LORA = 256
ROPE_THETA = 10000.0
EPS = 1e-6
D_IN = 2 * C_CONV + Q_LORA + KV_LORA + ROPE_DIM + 2 * D_MODEL

kernel_name = 'hybrid_conformer_mla_gated'


def _rms_norm(x, g):
    x32 = x.astype(jnp.float32)
    y = x32 * lax.rsqrt(jnp.mean(x32 * x32, axis=-1, keepdims=True) + EPS)
    return (y * g.astype(jnp.float32)).astype(x.dtype)


def _layer_norm(x, g, b):
    x32 = x.astype(jnp.float32)
    mu = jnp.mean(x32, axis=-1, keepdims=True)
    xc = x32 - mu
    y = xc * lax.rsqrt(jnp.mean(xc * xc, axis=-1, keepdims=True) + EPS)
    return (y * g.astype(jnp.float32) + b.astype(jnp.float32)).astype(x.dtype)


def _swiglu(h, w_gate, w_up, w_down):
    return (jax.nn.silu(h @ w_gate) * (h @ w_up)) @ w_down


def _rope(x, cos, sin):
    x1, x2 = jnp.split(x.astype(jnp.float32), 2, axis=-1)
    return jnp.concatenate([x1 * cos - x2 * sin, x2 * cos + x1 * sin], axis=-1).astype(x.dtype)


def _conv_module(u2, conv_w, conv_b, ln_g, ln_b, w_conv_out):
    a, gate = jnp.split(u2, 2, axis=-1)
    u = a * jax.nn.sigmoid(gate)
    u = lax.conv_general_dilated(
        u, conv_w[:, None, :], window_strides=(1,), padding=[(CONV_WIDTH - 1, 0)],
        dimension_numbers=('NWC', 'WIO', 'NWC'), feature_group_count=C_CONV) + conv_b
    u = jax.nn.silu(_layer_norm(u, ln_g, ln_b))
    return u @ w_conv_out


def _chunk_causal_attention(q, k, v):
    b, s, h, dq = q.shape
    nb = s // Q_BLOCK
    scale = dq ** -0.5
    key_chunk = jnp.arange(s) // CHUNK
    q_blocks = q.reshape(b, nb, Q_BLOCK, h, dq).transpose(1, 0, 2, 3, 4)

    def one_block(args):
        qi, i = args
        scores = jnp.einsum('bqhd,bkhd->bhqk', qi, k).astype(jnp.float32) * scale
        q_chunk = (i * Q_BLOCK + jnp.arange(Q_BLOCK)) // CHUNK
        mask = key_chunk[None, :] <= q_chunk[:, None]
        p = jax.nn.softmax(jnp.where(mask, scores, -jnp.inf), axis=-1)
        return jnp.einsum('bhqk,bkhd->bqhd', p.astype(v.dtype), v)

    out = lax.map(one_block, (q_blocks, jnp.arange(nb)))
    return out.transpose(1, 0, 2, 3, 4).reshape(b, s, h, v.shape[-1])


def _mla(c_q, c_kv, k_rope, cos, sin, cq_norm, ckv_norm, w_uq, w_ukv, q_norm, k_norm, w_mla_out):
    b, s, _ = c_q.shape
    q = (_rms_norm(c_q, cq_norm) @ w_uq).reshape(b, s, N_HEADS, QK_DIM)
    kv = (_rms_norm(c_kv, ckv_norm) @ w_ukv).reshape(b, s, N_HEADS, NOPE_DIM + V_DIM)
    k_nope, v = kv[..., :NOPE_DIM], kv[..., NOPE_DIM:]
    k_r = jnp.broadcast_to(k_rope[:, :, None, :], (b, s, N_HEADS, ROPE_DIM))
    k = jnp.concatenate([k_nope, k_r], axis=-1)
    q = _rms_norm(q, q_norm)
    k = _rms_norm(k, k_norm)
    q = jnp.concatenate([q[..., :NOPE_DIM], _rope(q[..., NOPE_DIM:], cos, sin)], axis=-1)
    k = jnp.concatenate([k[..., :NOPE_DIM], _rope(k[..., NOPE_DIM:], cos, sin)], axis=-1)
    o = _chunk_causal_attention(q, k, v)
    return o.reshape(b, s, N_HEADS * V_DIM) @ w_mla_out


def setup_inputs(seed: int = 0) -> dict:
    key = jax.random.key(seed)
    ks = jax.random.split(key, 32)

    def nrm(k, shape, fan_in):
        return jax.random.normal(k, shape, jnp.float32) * fan_in ** -0.5

    def gain(k, shape):
        return 1.0 + 0.05 * jax.random.normal(k, shape, jnp.float32)

    def small(k, shape, s):
        return s * jax.random.normal(k, shape, jnp.float32)

    x = jax.random.normal(ks[0], (BATCH, SEQ, D_MODEL), jnp.float32)
    offset = jax.random.randint(ks[1], (BATCH, 1), 0, 1024, dtype=jnp.int32)
    positions = offset + jnp.arange(SEQ, dtype=jnp.int32)[None, :]
    return {
        'x': x,
        'positions': positions,
        'ffn1_norm': gain(ks[2], (DEPTH, D_MODEL)),
        'ffn1_w_gate': nrm(ks[3], (DEPTH, D_MODEL, D_FF), D_MODEL),
        'ffn1_w_up': nrm(ks[4], (DEPTH, D_MODEL, D_FF), D_MODEL),
        'ffn1_w_down': nrm(ks[5], (DEPTH, D_FF, D_MODEL), D_FF),
        'mix_norm': gain(ks[6], (DEPTH, D_MODEL)),
        'w_in': nrm(ks[7], (DEPTH, D_MODEL, D_IN), D_MODEL),
        'gate_bias': small(ks[8], (DEPTH, 2, D_MODEL), 0.1),
        'conv_w': nrm(ks[9], (DEPTH, CONV_WIDTH, C_CONV), CONV_WIDTH),
        'conv_b': small(ks[10], (DEPTH, C_CONV), 0.02),
        'conv_ln_g': gain(ks[11], (DEPTH, C_CONV)),
        'conv_ln_b': small(ks[12], (DEPTH, C_CONV), 0.02),
        'w_conv_out': nrm(ks[13], (DEPTH, C_CONV, D_MODEL), C_CONV),
        'cq_norm': gain(ks[14], (DEPTH, Q_LORA)),
        'ckv_norm': gain(ks[15], (DEPTH, KV_LORA)),
        'w_uq': nrm(ks[16], (DEPTH, Q_LORA, N_HEADS * QK_DIM), Q_LORA),
        'w_ukv': nrm(ks[17], (DEPTH, KV_LORA, N_HEADS * (NOPE_DIM + V_DIM)), KV_LORA),
        'q_norm': gain(ks[18], (DEPTH, QK_DIM)),
        'k_norm': gain(ks[19], (DEPTH, QK_DIM)),
        'w_mla_out': nrm(ks[20], (DEPTH, N_HEADS * V_DIM, D_MODEL), N_HEADS * V_DIM),
        'w_out': nrm(ks[21], (DEPTH, D_MODEL, D_MODEL), D_MODEL),
        'ffn2_norm': gain(ks[22], (DEPTH, D_MODEL)),
        'ffn2_w_gate': nrm(ks[23], (DEPTH, D_MODEL, D_FF), D_MODEL),
        'ffn2_w_up': nrm(ks[24], (DEPTH, D_MODEL, D_FF), D_MODEL),
        'ffn2_w_down': nrm(ks[25], (DEPTH, D_FF, D_MODEL), D_FF),
    }


def reference(x, positions, ffn1_norm, ffn1_w_gate, ffn1_w_up, ffn1_w_down, mix_norm, w_in,
              gate_bias, conv_w, conv_b, conv_ln_g, conv_ln_b, w_conv_out, cq_norm, ckv_norm,
              w_uq, w_ukv, q_norm, k_norm, w_mla_out, w_out, ffn2_norm, ffn2_w_gate, ffn2_w_up,
              ffn2_w_down):
    b, s, _ = x.shape
    inv_freq = ROPE_THETA ** (-jnp.arange(0, ROPE_DIM, 2, dtype=jnp.float32) / ROPE_DIM)
    ang = positions.astype(jnp.float32)[..., None] * inv_freq
    cos = jnp.cos(ang)[:, :, None, :]
    sin = jnp.sin(ang)[:, :, None, :]
    o1 = 2 * C_CONV
    o2 = o1 + Q_LORA
    o3 = o2 + KV_LORA
    o4 = o3 + ROPE_DIM
    for l in range(DEPTH):
        x = x + 0.5 * _swiglu(_rms_norm(x, ffn1_norm[l]), ffn1_w_gate[l], ffn1_w_up[l], ffn1_w_down[l])
        h = _rms_norm(x, mix_norm[l])
        proj = h @ w_in[l]
        y_conv = _conv_module(proj[..., :o1], conv_w[l], conv_b[l], conv_ln_g[l], conv_ln_b[l],
                              w_conv_out[l])
        y_mla = _mla(proj[..., o1:o2], proj[..., o2:o3], proj[..., o3:o4], cos, sin,
                     cq_norm[l], ckv_norm[l], w_uq[l], w_ukv[l], q_norm[l], k_norm[l], w_mla_out[l])
        gates = jax.nn.sigmoid(proj[..., o4:].reshape(b, s, 2, D_MODEL) + gate_bias[l])
        y = gates[:, :, 0] * y_conv + gates[:, :, 1] * y_mla
        x = x + y @ w_out[l]
        x = x + 0.5 * _swiglu(_rms_norm(x, ffn2_norm[l]), ffn2_w_gate[l], ffn2_w_up[l], ffn2_w_down[l])
    return x
```

```python
import functools
import math

import jax
import jax.numpy as jnp
from jax import lax
from jax.experimental import pallas as pl
from jax.experimental.pallas import tpu as pltpu

D_MODEL = 1024
D_FF = 2816
C_CONV = 512
CONV_WIDTH = 31
N_HEADS = 8
NOPE_DIM = 64
ROPE_DIM = 32
QK_DIM = NOPE_DIM + ROPE_DIM
V_DIM = 64
Q_LORA = 256
KV_LORA = 256
CHUNK = 64
ROPE_THETA = 10000.0
EPS = 1e-6

LANES = 128
HEAD_PAD = LANES
HALF_ROPE = ROPE_DIM // 2
VMEM_LIMIT = 56 * 1024 * 1024

FF_CHUNK = 256
N_FF_CHUNKS = D_FF // FF_CHUNK
TM_FFN = 512
TM_PROJ = 512
TM_MIX = 256
TQ = 256
TK = 256
HALO = 32

F32 = jnp.float32
BF16 = jnp.bfloat16


def _const_spec(shape):
    zeros = (0,) * len(shape)
    return pl.BlockSpec(shape, lambda *_: zeros, pipeline_mode=pl.Buffered(1))


def _rms(x, g):
    ms = jnp.mean(x * x, axis=-1, keepdims=True)
    return x * lax.rsqrt(ms + EPS) * g


def _dot(a, b):
    return jnp.dot(a, b, preferred_element_type=F32)


def _rope_body(pos_ref, inv_ref, cos_ref, sin_ref):
    ang = pos_ref[...].astype(F32) * inv_ref[...]
    cos_ref[...] = jnp.cos(ang)
    sin_ref[...] = jnp.sin(ang)


def _rope_tables(positions):
    b, s = positions.shape
    t = b * s
    rows = t * HALF_ROPE // LANES
    inv_freq = ROPE_THETA ** (-jnp.arange(0, ROPE_DIM, 2, dtype=F32) / ROPE_DIM)
    pos_rep = jnp.repeat(positions.reshape(t), HALF_ROPE).reshape(rows, LANES)
    inv_rep = jnp.tile(inv_freq, LANES // HALF_ROPE).reshape(1, LANES)
    tr = 512
    cos, sin = pl.pallas_call(
        _rope_body,
        name="rope_tables",
        grid=(rows // tr,),
        in_specs=[pl.BlockSpec((tr, LANES), lambda i: (i, 0)),
                  pl.BlockSpec((1, LANES), lambda i: (0, 0))],
        out_specs=[pl.BlockSpec((tr, LANES), lambda i: (i, 0))] * 2,
        out_shape=[jax.ShapeDtypeStruct((rows, LANES), F32)] * 2,
    )(pos_rep, inv_rep)
    cos = cos.reshape(b, s, HALF_ROPE)
    sin = sin.reshape(b, s, HALF_ROPE)
    ones = jnp.ones((b, s, NOPE_DIM), F32)
    zeros_lo = jnp.zeros((b, s, NOPE_DIM), F32)
    zeros_hi = jnp.zeros((b, s, HEAD_PAD - QK_DIM), F32)
    cos_a = jnp.concatenate([ones, cos, cos, zeros_hi], axis=-1)
    sin_b = jnp.concatenate([zeros_lo, -sin, sin, zeros_hi], axis=-1)
    return cos_a, sin_b


def _ffn_value(x, g, wgu_ref, wd_ref):
    xn = _rms(x, g).astype(BF16)
    acc = None
    for c in range(N_FF_CHUNKS):
        gu = _dot(xn, wgu_ref[c])
        gate = gu[:, :FF_CHUNK]
        up = gu[:, FF_CHUNK:]
        a = (gate * jax.nn.sigmoid(gate) * up).astype(BF16)
        d = _dot(a, wd_ref[c])
        acc = d if acc is None else acc + d
    return x + 0.5 * acc


def _ffn_body(x_ref, g_ref, wgu_ref, wd_ref, o_ref):
    o_ref[...] = _ffn_value(x_ref[...], g_ref[...], wgu_ref, wd_ref)


def _ffn(x2, g, wgu, wd):
    t = x2.shape[0]
    return pl.pallas_call(
        _ffn_body,
        name="ffn",
        grid=(t // TM_FFN,),
        in_specs=[pl.BlockSpec((TM_FFN, D_MODEL), lambda i: (i, 0)),
                  _const_spec((1, D_MODEL)),
                  _const_spec((N_FF_CHUNKS, D_MODEL, 2 * FF_CHUNK)),
                  _const_spec((N_FF_CHUNKS, FF_CHUNK, D_MODEL))],
        out_specs=pl.BlockSpec((TM_FFN, D_MODEL), lambda i: (i, 0)),
        out_shape=jax.ShapeDtypeStruct((t, D_MODEL), F32),
        compiler_params=pltpu.CompilerParams(
            dimension_semantics=("arbitrary",), vmem_limit_bytes=VMEM_LIMIT),
    )(x2, g, wgu, wd)


def _prep_ffn_weights(w_gate, w_up, w_down):
    wg = w_gate.astype(BF16).reshape(D_MODEL, N_FF_CHUNKS, FF_CHUNK)
    wu = w_up.astype(BF16).reshape(D_MODEL, N_FF_CHUNKS, FF_CHUNK)
    wgu = jnp.concatenate([wg, wu], axis=-1).transpose(1, 0, 2)
    wd = w_down.astype(BF16).reshape(N_FF_CHUNKS, FF_CHUNK, D_MODEL)
    return wgu, wd


def _inproj_body(x_ref, mixg_ref, wa_ref, cqg_ref, ckvg_ref, wq_ref, wkv_ref,
                 ga_ref, gb_ref, gka_ref, gkb_ref, ca_ref, sb_ref,
                 u_ref, q_ref, k_ref, v_ref):
    h = _rms(x_ref[0], mixg_ref[...]).astype(BF16)
    proj = _dot(h, wa_ref[...])
    o1 = 2 * C_CONV
    o2 = o1 + Q_LORA
    o3 = o2 + KV_LORA
    u_ref[0] = proj[:, :C_CONV] * jax.nn.sigmoid(proj[:, C_CONV:o1])
    cqn = _rms(proj[:, o1:o2], cqg_ref[...]).astype(BF16)
    ckvn = _rms(proj[:, o2:o3], ckvg_ref[...]).astype(BF16)
    kra = proj[:, o3:o3 + HEAD_PAD]
    krb = proj[:, o3 + HEAD_PAD:o3 + 2 * HEAD_PAD]
    qq = _dot(cqn, wq_ref[...])
    kv = _dot(ckvn, wkv_ref[...])
    ca = ca_ref[0]
    sb = sb_ref[0]
    q_cos = ga_ref[...] * ca
    q_sin = gb_ref[...] * sb
    gka = gka_ref[...]
    k_rot = kra * (gka * ca) + krb * (gkb_ref[...] * sb)
    ss_rope = jnp.sum(kra * kra, axis=-1, keepdims=True)
    hw = N_HEADS * HEAD_PAD
    for hh in range(N_HEADS):
        lo = hh * HEAD_PAD
        qa = qq[:, lo:lo + HEAD_PAD]
        qb = qq[:, hw + lo:hw + lo + HEAD_PAD]
        rq = lax.rsqrt(jnp.sum(qa * qa, axis=-1, keepdims=True) / QK_DIM + EPS)
        q_ref[0, hh] = (rq * (qa * q_cos + qb * q_sin)).astype(BF16)
        kn = kv[:, lo:lo + HEAD_PAD]
        ss = jnp.sum(kn * kn, axis=-1, keepdims=True) + ss_rope
        rk = lax.rsqrt(ss / QK_DIM + EPS)
        k_ref[0, hh] = (rk * (kn * gka + k_rot)).astype(BF16)
    v_ref[0] = kv[:, hw:].astype(BF16)


def _inproj(x3, mixg, wa, cqg, ckvg, wq, wkv, ga, gb, gka, gkb, cos_a, sin_b):
    b, s, _ = x3.shape
    tm = TM_PROJ
    hv = N_HEADS * V_DIM
    tok = lambda w: pl.BlockSpec((1, tm, w), lambda bi, i: (bi, i, 0))
    head = pl.BlockSpec((1, N_HEADS, tm, HEAD_PAD), lambda bi, i: (bi, 0, i, 0))
    return pl.pallas_call(
        _inproj_body,
        name="inproj",
        grid=(b, s // tm),
        in_specs=[tok(D_MODEL), _const_spec(mixg.shape), _const_spec(wa.shape),
                  _const_spec(cqg.shape), _const_spec(ckvg.shape),
                  _const_spec(wq.shape), _const_spec(wkv.shape),
                  _const_spec(ga.shape), _const_spec(gb.shape),
                  _const_spec(gka.shape), _const_spec(gkb.shape),
                  tok(HEAD_PAD), tok(HEAD_PAD)],
        out_specs=[tok(C_CONV), head, head, tok(hv)],
        out_shape=[jax.ShapeDtypeStruct((b, s, C_CONV), F32),
                   jax.ShapeDtypeStruct((b, N_HEADS, s, HEAD_PAD), BF16),
                   jax.ShapeDtypeStruct((b, N_HEADS, s, HEAD_PAD), BF16),
                   jax.ShapeDtypeStruct((b, s, hv), BF16)],
        compiler_params=pltpu.CompilerParams(
            dimension_semantics=("arbitrary", "arbitrary"),
            vmem_limit_bytes=VMEM_LIMIT),
    )(x3, mixg, wa, cqg, ckvg, wq, wkv, ga, gb, gka, gkb, cos_a, sin_b)


def _pad_lanes(w, lo, total=HEAD_PAD):
    pad = [(0, 0)] * (w.ndim - 1) + [(lo, total - lo - w.shape[-1])]
    return jnp.pad(w, pad)


def _swap_halves(w):
    return jnp.concatenate([w[..., HALF_ROPE:], w[..., :HALF_ROPE]], axis=-1)


def _prep_mixer_weights(w_in, w_uq, w_ukv, q_norm, k_norm):
    o1 = 2 * C_CONV
    o3 = o1 + Q_LORA + KV_LORA
    o4 = o3 + ROPE_DIM
    w_rope = w_in[:, o3:o4]
    wa = jnp.concatenate(
        [w_in[:, :o3], _pad_lanes(w_rope, NOPE_DIM),
         _pad_lanes(_swap_halves(w_rope), NOPE_DIM)], axis=-1).astype(BF16)
    w_gates = w_in[:, o4:].astype(BF16)

    wq3 = w_uq.reshape(Q_LORA, N_HEADS, QK_DIM)
    wq_a = _pad_lanes(wq3, 0).reshape(Q_LORA, N_HEADS * HEAD_PAD)
    wq_b = _pad_lanes(_swap_halves(wq3[..., NOPE_DIM:]), NOPE_DIM)
    wq_b = wq_b.reshape(Q_LORA, N_HEADS * HEAD_PAD)
    wq = jnp.concatenate([wq_a, wq_b], axis=-1).astype(BF16)

    wkv3 = w_ukv.reshape(KV_LORA, N_HEADS, NOPE_DIM + V_DIM)
    wk = _pad_lanes(wkv3[..., :NOPE_DIM], 0).reshape(KV_LORA, N_HEADS * HEAD_PAD)
    wv = wkv3[..., NOPE_DIM:].reshape(KV_LORA, N_HEADS * V_DIM)
    wkv = jnp.concatenate([wk, wv], axis=-1).astype(BF16)

    q_scale = QK_DIM ** -0.5 * math.log2(math.e)
    ga = _pad_lanes(q_norm * q_scale, 0).reshape(1, HEAD_PAD)
    gb = _pad_lanes(_swap_halves(q_norm[NOPE_DIM:]) * q_scale, NOPE_DIM).reshape(1, HEAD_PAD)
    gka = _pad_lanes(k_norm, 0).reshape(1, HEAD_PAD)
    gkb = _pad_lanes(_swap_halves(k_norm[NOPE_DIM:]), NOPE_DIM).reshape(1, HEAD_PAD)
    return wa, w_gates, wq, wkv, ga, gb, gka, gkb


def _attn_body(q_ref, k_ref, v_ref, o_ref, m_ref, l_ref, acc_ref):
    i = pl.program_id(2)
    row_chunk = lax.broadcasted_iota(jnp.int32, (TQ, TK), 0) // CHUNK
    col_chunk = lax.broadcasted_iota(jnp.int32, (TQ, TK), 1) // CHUNK
    diag_mask = col_chunk <= row_chunk

    def scores(hh, start):
        kb = k_ref[0, hh, pl.ds(start, TK), :]
        return lax.dot_general(q_ref[0, hh], kb, (((1,), (1,)), ((), ())),
                               preferred_element_type=F32)

    d0 = pl.multiple_of(i * TQ, TQ)
    for hh in range(2):
        s = jnp.where(diag_mask, scores(hh, d0), -jnp.inf)
        m = jnp.max(s, axis=-1, keepdims=True)
        p = jnp.exp2(s - m)
        m_ref[hh] = m
        l_ref[hh] = jnp.sum(p, axis=-1, keepdims=True)
        acc_ref[hh] = _dot(p.astype(BF16), v_ref[0, pl.ds(d0, TK), :])

    def step(j, carry):
        start = pl.multiple_of(j * TK, TK)
        vb = v_ref[0, pl.ds(start, TK), :]
        for hh in range(2):
            s = scores(hh, start)
            m_old = m_ref[hh]
            m_new = jnp.maximum(m_old, jnp.max(s, axis=-1, keepdims=True))
            alpha = jnp.exp2(m_old - m_new)
            p = jnp.exp2(s - m_new)
            l_ref[hh] = alpha * l_ref[hh] + jnp.sum(p, axis=-1, keepdims=True)
            acc_ref[hh] = alpha * acc_ref[hh] + _dot(p.astype(BF16), vb)
            m_ref[hh] = m_new
        return carry

    lax.fori_loop(0, i, step, 0)

    lane = lax.broadcasted_iota(jnp.int32, (TQ, LANES), 1)
    o0 = acc_ref[0] / l_ref[0]
    o1 = acc_ref[1] / l_ref[1]
    o_ref[0] = jnp.where(lane < V_DIM, o0, o1).astype(BF16)


def _attention(q, k, v):
    b, _, s, _ = q.shape
    return pl.pallas_call(
        _attn_body,
        name="attn",
        grid=(b, N_HEADS // 2, s // TQ),
        in_specs=[pl.BlockSpec((1, 2, TQ, HEAD_PAD), lambda bi, p, i: (bi, p, i, 0)),
                  pl.BlockSpec((1, 2, s, HEAD_PAD), lambda bi, p, i: (bi, p, 0, 0)),
                  pl.BlockSpec((1, s, LANES), lambda bi, p, i: (bi, 0, p))],
        out_specs=pl.BlockSpec((1, TQ, LANES), lambda bi, p, i: (bi, i, p)),
        out_shape=jax.ShapeDtypeStruct((b, s, N_HEADS * V_DIM), BF16),
        scratch_shapes=[pltpu.VMEM((2, TQ, 1), F32), pltpu.VMEM((2, TQ, 1), F32),
                        pltpu.VMEM((2, TQ, LANES), F32)],
        compiler_params=pltpu.CompilerParams(
            dimension_semantics=("arbitrary", "arbitrary", "arbitrary"),
            vmem_limit_bytes=VMEM_LIMIT),
    )(q, k, v)


def _mix_body(x_ref, u_ref, halo_ref, o_ref, mixg_ref, wg_ref, gbias_ref,
              cw_ref, cb_ref, lng_ref, lnb_ref, wco_ref, wmo_ref, wout_ref,
              out_ref, ubuf_ref):
    tm = TM_MIX
    i = pl.program_id(1)
    x = x_ref[0]
    h = _rms(x, mixg_ref[...]).astype(BF16)
    gates = jax.nn.sigmoid(_dot(h, wg_ref[...]) + gbias_ref[...])

    halo = halo_ref[0]
    ubuf_ref[0:HALO, :] = jnp.where(i > 0, halo, jnp.zeros_like(halo))
    ubuf_ref[HALO:HALO + tm, :] = u_ref[0]
    first = HALO - (CONV_WIDTH - 1)
    conv = None
    for kk in range(CONV_WIDTH):
        tap = cw_ref[kk:kk + 1, :] * ubuf_ref[first + kk:first + kk + tm, :]
        conv = tap if conv is None else conv + tap
    conv = conv + cb_ref[...]
    mu = jnp.mean(conv, axis=-1, keepdims=True)
    cc = conv - mu
    ln = cc * lax.rsqrt(jnp.mean(cc * cc, axis=-1, keepdims=True) + EPS)
    ln = ln * lng_ref[...] + lnb_ref[...]
    act = (ln * jax.nn.sigmoid(ln)).astype(BF16)
    y_conv = _dot(act, wco_ref[...])
    y_mla = _dot(o_ref[0], wmo_ref[...])
    y = gates[:, :D_MODEL] * y_conv + gates[:, D_MODEL:] * y_mla
    out_ref[0] = x + _dot(y.astype(BF16), wout_ref[...])


def _mix(x3, u, o, mixg, w_gates, gbias, cw, cb, lng, lnb, wco, wmo, wout):
    b, s, _ = x3.shape
    tm = TM_MIX
    tok = lambda w: pl.BlockSpec((1, tm, w), lambda bi, i: (bi, i, 0))
    halo = pl.BlockSpec(
        (1, HALO, C_CONV), lambda bi, i: (bi, jnp.maximum(i * (tm // HALO) - 1, 0), 0))
    consts = [mixg, w_gates, gbias, cw, cb, lng, lnb, wco, wmo, wout]
    return pl.pallas_call(
        _mix_body,
        name="mix",
        grid=(b, s // tm),
        in_specs=[tok(D_MODEL), tok(C_CONV), halo, tok(N_HEADS * V_DIM)]
                 + [_const_spec(c.shape) for c in consts],
        out_specs=tok(D_MODEL),
        out_shape=jax.ShapeDtypeStruct((b, s, D_MODEL), F32),
        scratch_shapes=[pltpu.VMEM((HALO + tm, C_CONV), F32)],
        compiler_params=pltpu.CompilerParams(
            dimension_semantics=("arbitrary", "arbitrary"),
            vmem_limit_bytes=VMEM_LIMIT),
    )(x3, u, u, o, *consts)


def kernel(x, positions, ffn1_norm, ffn1_w_gate, ffn1_w_up, ffn1_w_down, mix_norm, w_in,
           gate_bias, conv_w, conv_b, conv_ln_g, conv_ln_b, w_conv_out, cq_norm, ckv_norm,
           w_uq, w_ukv, q_norm, k_norm, w_mla_out, w_out, ffn2_norm, ffn2_w_gate, ffn2_w_up,
           ffn2_w_down):
    b, s, d = x.shape
    depth = ffn1_norm.shape[0]
    assert d == D_MODEL and s % max(TM_PROJ, TM_MIX, TQ) == 0 and (b * s) % TM_FFN == 0
    row = lambda v: v.reshape(1, -1)
    cos_a, sin_b = _rope_tables(positions)
    for l in range(depth):
        wgu, wd = _prep_ffn_weights(ffn1_w_gate[l], ffn1_w_up[l], ffn1_w_down[l])
        x = _ffn(x.reshape(b * s, d), row(ffn1_norm[l]), wgu, wd).reshape(b, s, d)

        wa, w_gates, wq, wkv, ga, gb, gka, gkb = _prep_mixer_weights(
            w_in[l], w_uq[l], w_ukv[l], q_norm[l], k_norm[l])
        mixg = row(mix_norm[l])
        u, q, k, v = _inproj(x, mixg, wa, row(cq_norm[l]), row(ckv_norm[l]), wq, wkv,
                             ga, gb, gka, gkb, cos_a, sin_b)
        o = _attention(q, k, v)
        x = _mix(x, u, o, mixg, w_gates, gate_bias[l].reshape(1, 2 * D_MODEL),
                 conv_w[l], row(conv_b[l]), row(conv_ln_g[l]), row(conv_ln_b[l]),
                 w_conv_out[l].astype(BF16), w_mla_out[l].astype(BF16),
                 w_out[l].astype(BF16))

        wgu, wd = _prep_ffn_weights(ffn2_w_gate[l], ffn2_w_up[l], ffn2_w_down[l])
        x = _ffn(x.reshape(b * s, d), row(ffn2_norm[l]), wgu, wd).reshape(b, s, d)
    return x
```

```python
import functools
import math

import jax
import jax.numpy as jnp
from jax import lax
from jax.experimental import pallas as pl
from jax.experimental.pallas import tpu as pltpu

D_MODEL = 1024
D_FF = 2816
C_CONV = 512
CONV_WIDTH = 31
N_HEADS = 8
NOPE_DIM = 64
ROPE_DIM = 32
QK_DIM = NOPE_DIM + ROPE_DIM
V_DIM = 64
Q_LORA = 256
KV_LORA = 256
CHUNK = 64
ROPE_THETA = 10000.0
EPS = 1e-6

LANES = 128
HEAD_PAD = LANES
HALF_ROPE = ROPE_DIM // 2
VMEM_LIMIT = 56 * 1024 * 1024

FF_CHUNK = 256
N_FF_CHUNKS = D_FF // FF_CHUNK
TM_FFN = 512
TM_PROJ = 512
TM_MIX = 256
TQ = 512
TK = TM_PROJ
HALO = 32

F32 = jnp.float32
BF16 = jnp.bfloat16


def _const_spec(shape):
    zeros = (0,) * len(shape)
    return pl.BlockSpec(shape, lambda *_: zeros, pipeline_mode=pl.Buffered(1))


def _rms(x, g):
    ms = jnp.mean(x * x, axis=-1, keepdims=True)
    return x * lax.rsqrt(ms + EPS) * g


def _dot(a, b):
    return jnp.dot(a, b, preferred_element_type=F32)


def _rope_body(pos_ref, inv_ref, cos_ref, sin_ref):
    ang = pos_ref[...].astype(F32) * inv_ref[...]
    cos_ref[...] = jnp.cos(ang)
    sin_ref[...] = jnp.sin(ang)


def _rope_tables(positions):
    b, s = positions.shape
    t = b * s
    rows = t * HALF_ROPE // LANES
    inv_freq = ROPE_THETA ** (-jnp.arange(0, ROPE_DIM, 2, dtype=F32) / ROPE_DIM)
    pos_rep = jnp.repeat(positions.reshape(t), HALF_ROPE).reshape(rows, LANES)
    inv_rep = jnp.tile(inv_freq, LANES // HALF_ROPE).reshape(1, LANES)
    tr = 512
    cos, sin = pl.pallas_call(
        _rope_body,
        name="rope_tables",
        grid=(rows // tr,),
        in_specs=[pl.BlockSpec((tr, LANES), lambda i: (i, 0)),
                  pl.BlockSpec((1, LANES), lambda i: (0, 0))],
        out_specs=[pl.BlockSpec((tr, LANES), lambda i: (i, 0))] * 2,
        out_shape=[jax.ShapeDtypeStruct((rows, LANES), F32)] * 2,
    )(pos_rep, inv_rep)
    cos = cos.reshape(b, s, HALF_ROPE)
    sin = sin.reshape(b, s, HALF_ROPE)
    ones = jnp.ones((b, s, NOPE_DIM), F32)
    zeros_lo = jnp.zeros((b, s, NOPE_DIM), F32)
    zeros_hi = jnp.zeros((b, s, HEAD_PAD - QK_DIM), F32)
    cos_a = jnp.concatenate([ones, cos, cos, zeros_hi], axis=-1)
    sin_b = jnp.concatenate([zeros_lo, -sin, sin, zeros_hi], axis=-1)
    return cos_a, sin_b


def _ffn_value(x, g, wgu_ref, wd_ref):
    xn = _rms(x, g).astype(BF16)
    acc = None
    for c in range(N_FF_CHUNKS):
        gu = _dot(xn, wgu_ref[c])
        gate = gu[:, :FF_CHUNK]
        up = gu[:, FF_CHUNK:]
        a = (gate * jax.nn.sigmoid(gate) * up).astype(BF16)
        d = _dot(a, wd_ref[c])
        acc = d if acc is None else acc + d
    return x + 0.5 * acc


def _ffn_body(x_ref, g_ref, wgu_ref, wd_ref, o_ref):
    o_ref[...] = _ffn_value(x_ref[...], g_ref[...], wgu_ref, wd_ref)


def _ffn(x2, g, wgu, wd):
    t = x2.shape[0]
    return pl.pallas_call(
        _ffn_body,
        name="ffn",
        grid=(t // TM_FFN,),
        in_specs=[pl.BlockSpec((TM_FFN, D_MODEL), lambda i: (i, 0)),
                  _const_spec((1, D_MODEL)),
                  _const_spec((N_FF_CHUNKS, D_MODEL, 2 * FF_CHUNK)),
                  _const_spec((N_FF_CHUNKS, FF_CHUNK, D_MODEL))],
        out_specs=pl.BlockSpec((TM_FFN, D_MODEL), lambda i: (i, 0)),
        out_shape=jax.ShapeDtypeStruct((t, D_MODEL), F32),
        compiler_params=pltpu.CompilerParams(
            dimension_semantics=("arbitrary",), vmem_limit_bytes=VMEM_LIMIT),
    )(x2, g, wgu, wd)


def _prep_ffn_weights(w_gate, w_up, w_down):
    wg = w_gate.astype(BF16).reshape(D_MODEL, N_FF_CHUNKS, FF_CHUNK)
    wu = w_up.astype(BF16).reshape(D_MODEL, N_FF_CHUNKS, FF_CHUNK)
    wgu = jnp.concatenate([wg, wu], axis=-1).transpose(1, 0, 2)
    wd = w_down.astype(BF16).reshape(N_FF_CHUNKS, FF_CHUNK, D_MODEL)
    return wgu, wd


def _inproj_body(x_ref, mixg_ref, wa_ref, cqg_ref, ckvg_ref, wq_ref, wk_ref, wvt_ref,
                 ga_ref, gb_ref, gka_ref, gkb_ref, ca_ref, sb_ref,
                 u_ref, q_ref, k_ref, vt_ref):
    h = _rms(x_ref[0], mixg_ref[...]).astype(BF16)
    proj = _dot(h, wa_ref[...])
    o1 = 2 * C_CONV
    o2 = o1 + Q_LORA
    o3 = o2 + KV_LORA
    u_ref[0] = proj[:, :C_CONV] * jax.nn.sigmoid(proj[:, C_CONV:o1])
    cqn = _rms(proj[:, o1:o2], cqg_ref[...]).astype(BF16)
    ckvn = _rms(proj[:, o2:o3], ckvg_ref[...]).astype(BF16)
    kra = proj[:, o3:o3 + HEAD_PAD]
    krb = proj[:, o3 + HEAD_PAD:o3 + 2 * HEAD_PAD]
    qq = _dot(cqn, wq_ref[...])
    kv = _dot(ckvn, wk_ref[...])
    vt_ref[0, 0] = lax.dot_general(wvt_ref[...], ckvn, (((1,), (1,)), ((), ())),
                                   preferred_element_type=F32).astype(BF16)
    ca = ca_ref[0]
    sb = sb_ref[0]
    q_cos = ga_ref[...] * ca
    q_sin = gb_ref[...] * sb
    gka = gka_ref[...]
    k_rot = kra * (gka * ca) + krb * (gkb_ref[...] * sb)
    ss_rope = jnp.sum(kra * kra, axis=-1, keepdims=True)
    hw = N_HEADS * HEAD_PAD
    for hh in range(N_HEADS):
        lo = hh * HEAD_PAD
        qa = qq[:, lo:lo + HEAD_PAD]
        qb = qq[:, hw + lo:hw + lo + HEAD_PAD]
        rq = lax.rsqrt(jnp.sum(qa * qa, axis=-1, keepdims=True) / QK_DIM + EPS)
        q_ref[0, hh] = (rq * (qa * q_cos + qb * q_sin)).astype(BF16)
        kn = kv[:, lo:lo + HEAD_PAD]
        ss = jnp.sum(kn * kn, axis=-1, keepdims=True) + ss_rope
        rk = lax.rsqrt(ss / QK_DIM + EPS)
        k_ref[0, hh] = (rk * (kn * gka + k_rot)).astype(BF16)


def _inproj(x3, mixg, wa, cqg, ckvg, wq, wk, wvt, ga, gb, gka, gkb, cos_a, sin_b):
    b, s, _ = x3.shape
    tm = TM_PROJ
    hv = N_HEADS * V_DIM
    tok = lambda w: pl.BlockSpec((1, tm, w), lambda bi, i: (bi, i, 0))
    head = pl.BlockSpec((1, N_HEADS, tm, HEAD_PAD), lambda bi, i: (bi, 0, i, 0))
    vt_spec = pl.BlockSpec((1, 1, hv, tm), lambda bi, i: (bi, i, 0, 0))
    return pl.pallas_call(
        _inproj_body,
        name="inproj",
        grid=(b, s // tm),
        in_specs=[tok(D_MODEL), _const_spec(mixg.shape), _const_spec(wa.shape),
                  _const_spec(cqg.shape), _const_spec(ckvg.shape),
                  _const_spec(wq.shape), _const_spec(wk.shape), _const_spec(wvt.shape),
                  _const_spec(ga.shape), _const_spec(gb.shape),
                  _const_spec(gka.shape), _const_spec(gkb.shape),
                  tok(HEAD_PAD), tok(HEAD_PAD)],
        out_specs=[tok(C_CONV), head, head, vt_spec],
        out_shape=[jax.ShapeDtypeStruct((b, s, C_CONV), F32),
                   jax.ShapeDtypeStruct((b, N_HEADS, s, HEAD_PAD), BF16),
                   jax.ShapeDtypeStruct((b, N_HEADS, s, HEAD_PAD), BF16),
                   jax.ShapeDtypeStruct((b, s // tm, hv, tm), BF16)],
        compiler_params=pltpu.CompilerParams(
            dimension_semantics=("arbitrary", "arbitrary"),
            vmem_limit_bytes=VMEM_LIMIT),
    )(x3, mixg, wa, cqg, ckvg, wq, wk, wvt, ga, gb, gka, gkb, cos_a, sin_b)


def _pad_lanes(w, lo, total=HEAD_PAD):
    pad = [(0, 0)] * (w.ndim - 1) + [(lo, total - lo - w.shape[-1])]
    return jnp.pad(w, pad)


def _swap_halves(w):
    return jnp.concatenate([w[..., HALF_ROPE:], w[..., :HALF_ROPE]], axis=-1)


def _prep_mixer_weights(w_in, w_uq, w_ukv, q_norm, k_norm):
    o1 = 2 * C_CONV
    o3 = o1 + Q_LORA + KV_LORA
    o4 = o3 + ROPE_DIM
    w_rope = w_in[:, o3:o4]
    wa = jnp.concatenate(
        [w_in[:, :o3], _pad_lanes(w_rope, NOPE_DIM),
         _pad_lanes(_swap_halves(w_rope), NOPE_DIM)], axis=-1).astype(BF16)
    w_gates = w_in[:, o4:].astype(BF16)

    wq3 = w_uq.reshape(Q_LORA, N_HEADS, QK_DIM)
    wq_a = _pad_lanes(wq3, 0).reshape(Q_LORA, N_HEADS * HEAD_PAD)
    wq_b = _pad_lanes(_swap_halves(wq3[..., NOPE_DIM:]), NOPE_DIM)
    wq_b = wq_b.reshape(Q_LORA, N_HEADS * HEAD_PAD)
    wq = jnp.concatenate([wq_a, wq_b], axis=-1).astype(BF16)

    wkv3 = w_ukv.reshape(KV_LORA, N_HEADS, NOPE_DIM + V_DIM)
    wk = _pad_lanes(wkv3[..., :NOPE_DIM], 0).reshape(KV_LORA, N_HEADS * HEAD_PAD)
    wk = wk.astype(BF16)
    wvt = wkv3[..., NOPE_DIM:].reshape(KV_LORA, N_HEADS * V_DIM).T.astype(BF16)

    q_scale = QK_DIM ** -0.5 * math.log2(math.e)
    ga = _pad_lanes(q_norm * q_scale, 0).reshape(1, HEAD_PAD)
    gb = _pad_lanes(_swap_halves(q_norm[NOPE_DIM:]) * q_scale, NOPE_DIM).reshape(1, HEAD_PAD)
    gka = _pad_lanes(k_norm, 0).reshape(1, HEAD_PAD)
    gkb = _pad_lanes(_swap_halves(k_norm[NOPE_DIM:]), NOPE_DIM).reshape(1, HEAD_PAD)
    return wa, w_gates, wq, wk, wvt, ga, gb, gka, gkb


def _attn_body(q_ref, k_ref, vt_ref, o_ref, m_ref, l_ref, acc_ref):
    i = pl.program_id(2)
    key_chunk = lax.broadcasted_iota(jnp.int32, (TK, TQ), 0) // CHUNK
    qry_chunk = lax.broadcasted_iota(jnp.int32, (TK, TQ), 1) // CHUNK
    diag_mask = key_chunk <= qry_chunk

    def scores_t(hh, j):
        kb = k_ref[0, hh, pl.ds(pl.multiple_of(j * TK, TK), TK), :]
        return lax.dot_general(kb, q_ref[0, hh], (((1,), (1,)), ((), ())),
                               preferred_element_type=F32)

    def values_t(hh, j):
        return vt_ref[0, j, hh * V_DIM:(hh + 1) * V_DIM, :]

    for hh in range(2):
        s = jnp.where(diag_mask, scores_t(hh, i), -jnp.inf)
        m = jnp.max(s, axis=0, keepdims=True)
        p = jnp.exp2(s - m)
        m_ref[hh] = m
        l_ref[hh] = jnp.sum(p, axis=0, keepdims=True)
        acc_ref[hh] = _dot(values_t(hh, i), p.astype(BF16))

    def step(j, carry):
        for hh in range(2):
            s = scores_t(hh, j)
            m_old = m_ref[hh]
            m_new = jnp.maximum(m_old, jnp.max(s, axis=0, keepdims=True))
            alpha = jnp.exp2(m_old - m_new)
            p = jnp.exp2(s - m_new)
            l_ref[hh] = alpha * l_ref[hh] + jnp.sum(p, axis=0, keepdims=True)
            acc_ref[hh] = alpha * acc_ref[hh] + _dot(values_t(hh, j), p.astype(BF16))
            m_ref[hh] = m_new
        return carry

    lax.fori_loop(0, i, step, 0)

    o_t = jnp.concatenate([acc_ref[0] / l_ref[0], acc_ref[1] / l_ref[1]], axis=0)
    o_ref[0] = o_t.T.astype(BF16)


def _attention(q, k, vt):
    b, _, s, _ = q.shape
    assert TQ == TK
    return pl.pallas_call(
        _attn_body,
        name="attn",
        grid=(b, N_HEADS // 2, s // TQ),
        in_specs=[pl.BlockSpec((1, 2, TQ, HEAD_PAD), lambda bi, p, i: (bi, p, i, 0)),
                  pl.BlockSpec((1, 2, s, HEAD_PAD), lambda bi, p, i: (bi, p, 0, 0)),
                  pl.BlockSpec((1, s // TK, 2 * V_DIM, TK), lambda bi, p, i: (bi, 0, p, 0))],
        out_specs=pl.BlockSpec((1, TQ, 2 * V_DIM), lambda bi, p, i: (bi, i, p)),
        out_shape=jax.ShapeDtypeStruct((b, s, N_HEADS * V_DIM), BF16),
        scratch_shapes=[pltpu.VMEM((2, 1, TQ), F32), pltpu.VMEM((2, 1, TQ), F32),
                        pltpu.VMEM((2, V_DIM, TQ), F32)],
        compiler_params=pltpu.CompilerParams(
            dimension_semantics=("arbitrary", "arbitrary", "arbitrary"),
            vmem_limit_bytes=VMEM_LIMIT),
    )(q, k, vt)


def _mix_body(x_ref, u_ref, halo_ref, o_ref, mixg_ref, wg_ref, gbias_ref,
              cw_ref, cb_ref, lng_ref, lnb_ref, wco_ref, wmo_ref, wout_ref,
              out_ref, ubuf_ref):
    tm = TM_MIX
    i = pl.program_id(1)
    x = x_ref[0]
    h = _rms(x, mixg_ref[...]).astype(BF16)
    gates = jax.nn.sigmoid(_dot(h, wg_ref[...]) + gbias_ref[...])

    halo = halo_ref[0]
    ubuf_ref[0:HALO, :] = jnp.where(i > 0, halo, jnp.zeros_like(halo))
    ubuf_ref[HALO:HALO + tm, :] = u_ref[0]
    first = HALO - (CONV_WIDTH - 1)
    conv = None
    for kk in range(CONV_WIDTH):
        tap = cw_ref[kk:kk + 1, :] * ubuf_ref[first + kk:first + kk + tm, :]
        conv = tap if conv is None else conv + tap
    conv = conv + cb_ref[...]
    mu = jnp.mean(conv, axis=-1, keepdims=True)
    cc = conv - mu
    ln = cc * lax.rsqrt(jnp.mean(cc * cc, axis=-1, keepdims=True) + EPS)
    ln = ln * lng_ref[...] + lnb_ref[...]
    act = (ln * jax.nn.sigmoid(ln)).astype(BF16)
    y_conv = _dot(act, wco_ref[...])
    y_mla = _dot(o_ref[0], wmo_ref[...])
    y = gates[:, :D_MODEL] * y_conv + gates[:, D_MODEL:] * y_mla
    out_ref[0] = x + _dot(y.astype(BF16), wout_ref[...])


def _mix(x3, u, o, mixg, w_gates, gbias, cw, cb, lng, lnb, wco, wmo, wout):
    b, s, _ = x3.shape
    tm = TM_MIX
    tok = lambda w: pl.BlockSpec((1, tm, w), lambda bi, i: (bi, i, 0))
    halo = pl.BlockSpec(
        (1, HALO, C_CONV), lambda bi, i: (bi, jnp.maximum(i * (tm // HALO) - 1, 0), 0))
    consts = [mixg, w_gates, gbias, cw, cb, lng, lnb, wco, wmo, wout]
    return pl.pallas_call(
        _mix_body,
        name="mix",
        grid=(b, s // tm),
        in_specs=[tok(D_MODEL), tok(C_CONV), halo, tok(N_HEADS * V_DIM)]
                 + [_const_spec(c.shape) for c in consts],
        out_specs=tok(D_MODEL),
        out_shape=jax.ShapeDtypeStruct((b, s, D_MODEL), F32),
        scratch_shapes=[pltpu.VMEM((HALO + tm, C_CONV), F32)],
        compiler_params=pltpu.CompilerParams(
            dimension_semantics=("arbitrary", "arbitrary"),
            vmem_limit_bytes=VMEM_LIMIT),
    )(x3, u, u, o, *consts)


def kernel(x, positions, ffn1_norm, ffn1_w_gate, ffn1_w_up, ffn1_w_down, mix_norm, w_in,
           gate_bias, conv_w, conv_b, conv_ln_g, conv_ln_b, w_conv_out, cq_norm, ckv_norm,
           w_uq, w_ukv, q_norm, k_norm, w_mla_out, w_out, ffn2_norm, ffn2_w_gate, ffn2_w_up,
           ffn2_w_down):
    b, s, d = x.shape
    depth = ffn1_norm.shape[0]
    assert d == D_MODEL and s % max(TM_PROJ, TM_MIX, TQ) == 0 and (b * s) % TM_FFN == 0
    row = lambda v: v.reshape(1, -1)
    cos_a, sin_b = _rope_tables(positions)
    for l in range(depth):
        wgu, wd = _prep_ffn_weights(ffn1_w_gate[l], ffn1_w_up[l], ffn1_w_down[l])
        x = _ffn(x.reshape(b * s, d), row(ffn1_norm[l]), wgu, wd).reshape(b, s, d)

        wa, w_gates, wq, wk, wvt, ga, gb, gka, gkb = _prep_mixer_weights(
            w_in[l], w_uq[l], w_ukv[l], q_norm[l], k_norm[l])
        mixg = row(mix_norm[l])
        u, q, k, vt = _inproj(x, mixg, wa, row(cq_norm[l]), row(ckv_norm[l]), wq, wk, wvt,
                              ga, gb, gka, gkb, cos_a, sin_b)
        o = _attention(q, k, vt)
        x = _mix(x, u, o, mixg, w_gates, gate_bias[l].reshape(1, 2 * D_MODEL),
                 conv_w[l], row(conv_b[l]), row(conv_ln_g[l]), row(conv_ln_b[l]),
                 w_conv_out[l].astype(BF16), w_mla_out[l].astype(BF16),
                 w_out[l].astype(BF16))

        wgu, wd = _prep_ffn_weights(ffn2_w_gate[l], ffn2_w_up[l], ffn2_w_down[l])
        x = _ffn(x.reshape(b * s, d), row(ffn2_norm[l]), wgu, wd).reshape(b, s, d)
    return x
```

```python
import functools
import math

import jax
import jax.numpy as jnp
from jax import lax
from jax.experimental import pallas as pl
from jax.experimental.pallas import tpu as pltpu

D_MODEL = 1024
D_FF = 2816
C_CONV = 512
CONV_WIDTH = 31
N_HEADS = 8
NOPE_DIM = 64
ROPE_DIM = 32
QK_DIM = NOPE_DIM + ROPE_DIM
V_DIM = 64
Q_LORA = 256
KV_LORA = 256
CHUNK = 64
ROPE_THETA = 10000.0
EPS = 1e-6

LANES = 128
SUBLANES = 8
MAX_FIXED_OFFSET = 50.0
HEAD_PAD = LANES
HALF_ROPE = ROPE_DIM // 2
VMEM_LIMIT = 56 * 1024 * 1024

FF_CHUNK = 256
N_FF_CHUNKS = D_FF // FF_CHUNK
TM_FFN = 512
TM_PROJ = 512
TM_MIX = 256
TQ = 512
TK = TM_PROJ
HALO = 32

F32 = jnp.float32
BF16 = jnp.bfloat16


def _const_spec(shape):
    zeros = (0,) * len(shape)
    return pl.BlockSpec(shape, lambda *_: zeros, pipeline_mode=pl.Buffered(1))


def _rms(x, g):
    ms = jnp.mean(x * x, axis=-1, keepdims=True)
    return x * lax.rsqrt(ms + EPS) * g


def _dot(a, b):
    return jnp.dot(a, b, preferred_element_type=F32)


def _rope_body(pos_ref, inv_ref, cos_ref, sin_ref):
    ang = pos_ref[...].astype(F32) * inv_ref[...]
    cos_ref[...] = jnp.cos(ang)
    sin_ref[...] = jnp.sin(ang)


def _rope_tables(positions):
    b, s = positions.shape
    t = b * s
    rows = t * HALF_ROPE // LANES
    inv_freq = ROPE_THETA ** (-jnp.arange(0, ROPE_DIM, 2, dtype=F32) / ROPE_DIM)
    pos_rep = jnp.repeat(positions.reshape(t), HALF_ROPE).reshape(rows, LANES)
    inv_rep = jnp.tile(inv_freq, LANES // HALF_ROPE).reshape(1, LANES)
    tr = 512
    cos, sin = pl.pallas_call(
        _rope_body,
        name="rope_tables",
        grid=(rows // tr,),
        in_specs=[pl.BlockSpec((tr, LANES), lambda i: (i, 0)),
                  pl.BlockSpec((1, LANES), lambda i: (0, 0))],
        out_specs=[pl.BlockSpec((tr, LANES), lambda i: (i, 0))] * 2,
        out_shape=[jax.ShapeDtypeStruct((rows, LANES), F32)] * 2,
    )(pos_rep, inv_rep)
    cos = cos.reshape(b, s, HALF_ROPE)
    sin = sin.reshape(b, s, HALF_ROPE)
    ones = jnp.ones((b, s, NOPE_DIM), F32)
    zeros_lo = jnp.zeros((b, s, NOPE_DIM), F32)
    zeros_hi = jnp.zeros((b, s, HEAD_PAD - QK_DIM), F32)
    cos_a = jnp.concatenate([ones, cos, cos, zeros_hi], axis=-1)
    sin_b = jnp.concatenate([zeros_lo, -sin, sin, zeros_hi], axis=-1)
    return cos_a, sin_b


def _ffn_value(x, g, wgu_ref, wd_ref):
    xn = _rms(x, g).astype(BF16)
    acc = None
    for c in range(N_FF_CHUNKS):
        gu = _dot(xn, wgu_ref[c])
        gate = gu[:, :FF_CHUNK]
        up = gu[:, FF_CHUNK:]
        a = (gate * jax.nn.sigmoid(gate) * up).astype(BF16)
        d = _dot(a, wd_ref[c])
        acc = d if acc is None else acc + d
    return x + 0.5 * acc


def _ffn_body(x_ref, g_ref, wgu_ref, wd_ref, o_ref):
    o_ref[...] = _ffn_value(x_ref[...], g_ref[...], wgu_ref, wd_ref)


def _ffn(x2, g, wgu, wd):
    t = x2.shape[0]
    return pl.pallas_call(
        _ffn_body,
        name="ffn",
        grid=(t // TM_FFN,),
        in_specs=[pl.BlockSpec((TM_FFN, D_MODEL), lambda i: (i, 0)),
                  _const_spec((1, D_MODEL)),
                  _const_spec((N_FF_CHUNKS, D_MODEL, 2 * FF_CHUNK)),
                  _const_spec((N_FF_CHUNKS, FF_CHUNK, D_MODEL))],
        out_specs=pl.BlockSpec((TM_FFN, D_MODEL), lambda i: (i, 0)),
        out_shape=jax.ShapeDtypeStruct((t, D_MODEL), F32),
        compiler_params=pltpu.CompilerParams(
            dimension_semantics=("arbitrary",), vmem_limit_bytes=VMEM_LIMIT),
    )(x2, g, wgu, wd)


def _prep_ffn_weights(w_gate, w_up, w_down):
    wg = w_gate.astype(BF16).reshape(D_MODEL, N_FF_CHUNKS, FF_CHUNK)
    wu = w_up.astype(BF16).reshape(D_MODEL, N_FF_CHUNKS, FF_CHUNK)
    wgu = jnp.concatenate([wg, wu], axis=-1).transpose(1, 0, 2)
    wd = w_down.astype(BF16).reshape(N_FF_CHUNKS, FF_CHUNK, D_MODEL)
    return wgu, wd


def _inproj_body(x_ref, mixg_ref, wa_ref, cqg_ref, ckvg_ref, wq_ref, wk_ref, wvt_ref,
                 ga_ref, gb_ref, gka_ref, gkb_ref, qoff_ref, koff_ref, ca_ref, sb_ref,
                 u_ref, q_ref, k_ref, vt_ref):
    h = _rms(x_ref[0], mixg_ref[...]).astype(BF16)
    proj = _dot(h, wa_ref[...])
    o1 = 2 * C_CONV
    o2 = o1 + Q_LORA
    o3 = o2 + KV_LORA
    u_ref[0] = proj[:, :C_CONV] * jax.nn.sigmoid(proj[:, C_CONV:o1])
    cqn = _rms(proj[:, o1:o2], cqg_ref[...]).astype(BF16)
    ckvn = _rms(proj[:, o2:o3], ckvg_ref[...]).astype(BF16)
    kra = proj[:, o3:o3 + HEAD_PAD]
    krb = proj[:, o3 + HEAD_PAD:o3 + 2 * HEAD_PAD]
    qq = _dot(cqn, wq_ref[...])
    kv = _dot(ckvn, wk_ref[...])
    vt_ref[0, 0] = lax.dot_general(wvt_ref[...], ckvn, (((1,), (1,)), ((), ())),
                                   preferred_element_type=F32).astype(BF16)
    ca = ca_ref[0]
    sb = sb_ref[0]
    q_cos = ga_ref[...] * ca
    q_sin = gb_ref[...] * sb
    gka = gka_ref[...]
    k_rot = kra * (gka * ca) + krb * (gkb_ref[...] * sb)
    ss_rope = jnp.sum(kra * kra, axis=-1, keepdims=True)
    hw = N_HEADS * HEAD_PAD
    for hh in range(N_HEADS):
        lo = hh * HEAD_PAD
        qa = qq[:, lo:lo + HEAD_PAD]
        qb = qq[:, hw + lo:hw + lo + HEAD_PAD]
        rq = lax.rsqrt(jnp.sum(qa * qa, axis=-1, keepdims=True) / QK_DIM + EPS)
        q_ref[0, hh] = (rq * (qa * q_cos + qb * q_sin) + qoff_ref[...]).astype(BF16)
        kn = kv[:, lo:lo + HEAD_PAD]
        ss = jnp.sum(kn * kn, axis=-1, keepdims=True) + ss_rope
        rk = lax.rsqrt(ss / QK_DIM + EPS)
        k_ref[0, hh] = (rk * (kn * gka + k_rot) + koff_ref[...]).astype(BF16)


def _inproj(x3, mixg, wa, cqg, ckvg, wq, wk, wvt, ga, gb, gka, gkb, q_off, k_off,
            cos_a, sin_b):
    b, s, _ = x3.shape
    tm = TM_PROJ
    hv = N_HEADS * V_DIM
    tok = lambda w: pl.BlockSpec((1, tm, w), lambda bi, i: (bi, i, 0))
    head = pl.BlockSpec((1, N_HEADS, tm, HEAD_PAD), lambda bi, i: (bi, 0, i, 0))
    vt_spec = pl.BlockSpec((1, 1, hv, tm), lambda bi, i: (bi, i, 0, 0))
    return pl.pallas_call(
        _inproj_body,
        name="inproj",
        grid=(b, s // tm),
        in_specs=[tok(D_MODEL), _const_spec(mixg.shape), _const_spec(wa.shape),
                  _const_spec(cqg.shape), _const_spec(ckvg.shape),
                  _const_spec(wq.shape), _const_spec(wk.shape), _const_spec(wvt.shape),
                  _const_spec(ga.shape), _const_spec(gb.shape),
                  _const_spec(gka.shape), _const_spec(gkb.shape),
                  _const_spec(q_off.shape), _const_spec(k_off.shape),
                  tok(HEAD_PAD), tok(HEAD_PAD)],
        out_specs=[tok(C_CONV), head, head, vt_spec],
        out_shape=[jax.ShapeDtypeStruct((b, s, C_CONV), F32),
                   jax.ShapeDtypeStruct((b, N_HEADS, s, HEAD_PAD), BF16),
                   jax.ShapeDtypeStruct((b, N_HEADS, s, HEAD_PAD), BF16),
                   jax.ShapeDtypeStruct((b, s // tm, hv, tm), BF16)],
        compiler_params=pltpu.CompilerParams(
            dimension_semantics=("arbitrary", "arbitrary"),
            vmem_limit_bytes=VMEM_LIMIT),
    )(x3, mixg, wa, cqg, ckvg, wq, wk, wvt, ga, gb, gka, gkb, q_off, k_off, cos_a, sin_b)


def _pad_lanes(w, lo, total=HEAD_PAD):
    pad = [(0, 0)] * (w.ndim - 1) + [(lo, total - lo - w.shape[-1])]
    return jnp.pad(w, pad)


def _swap_halves(w):
    return jnp.concatenate([w[..., HALF_ROPE:], w[..., :HALF_ROPE]], axis=-1)


def _prep_mixer_weights(w_in, w_uq, w_ukv, q_norm, k_norm):
    o1 = 2 * C_CONV
    o3 = o1 + Q_LORA + KV_LORA
    o4 = o3 + ROPE_DIM
    w_rope = w_in[:, o3:o4]
    wa = jnp.concatenate(
        [w_in[:, :o3], _pad_lanes(w_rope, NOPE_DIM),
         _pad_lanes(_swap_halves(w_rope), NOPE_DIM)], axis=-1).astype(BF16)
    w_gates = w_in[:, o4:].astype(BF16)

    wq3 = w_uq.reshape(Q_LORA, N_HEADS, QK_DIM)
    wq_a = _pad_lanes(wq3, 0).reshape(Q_LORA, N_HEADS * HEAD_PAD)
    wq_b = _pad_lanes(_swap_halves(wq3[..., NOPE_DIM:]), NOPE_DIM)
    wq_b = wq_b.reshape(Q_LORA, N_HEADS * HEAD_PAD)
    wq = jnp.concatenate([wq_a, wq_b], axis=-1).astype(BF16)

    wkv3 = w_ukv.reshape(KV_LORA, N_HEADS, NOPE_DIM + V_DIM)
    wk = _pad_lanes(wkv3[..., :NOPE_DIM], 0).reshape(KV_LORA, N_HEADS * HEAD_PAD)
    wk = wk.astype(BF16)
    wvt = wkv3[..., NOPE_DIM:].reshape(KV_LORA, N_HEADS * V_DIM).T.astype(BF16)

    q_scale = QK_DIM ** -0.5 * math.log2(math.e)
    ga = _pad_lanes(q_norm * q_scale, 0).reshape(1, HEAD_PAD)
    gb = _pad_lanes(_swap_halves(q_norm[NOPE_DIM:]) * q_scale, NOPE_DIM).reshape(1, HEAD_PAD)
    gka = _pad_lanes(k_norm, 0).reshape(1, HEAD_PAD)
    gkb = _pad_lanes(_swap_halves(k_norm[NOPE_DIM:]), NOPE_DIM).reshape(1, HEAD_PAD)

    bound = QK_DIM * q_scale * jnp.max(jnp.abs(q_norm)) * jnp.max(jnp.abs(k_norm))
    bounded = bound <= MAX_FIXED_OFFSET
    lane = jnp.arange(HEAD_PAD) == QK_DIM
    q_off = jnp.where(lane & bounded, -bound, 0.0).astype(F32).reshape(1, HEAD_PAD)
    k_off = jnp.where(lane, 1.0, 0.0).astype(F32).reshape(1, HEAD_PAD)
    return wa, w_gates, wq, wk, wvt, ga, gb, gka, gkb, q_off, k_off, bounded


def _attn_body(q_ref, k_ref, vt_ref, o_ref, m_ref, l_ref, acc_ref):
    i = pl.program_id(2)
    key_chunk = lax.broadcasted_iota(jnp.int32, (TK, TQ), 0) // CHUNK
    qry_chunk = lax.broadcasted_iota(jnp.int32, (TK, TQ), 1) // CHUNK
    diag_mask = key_chunk <= qry_chunk

    def scores_t(hh, j):
        kb = k_ref[0, hh, pl.ds(pl.multiple_of(j * TK, TK), TK), :]
        return lax.dot_general(kb, q_ref[0, hh], (((1,), (1,)), ((), ())),
                               preferred_element_type=F32)

    def values_t(hh, j):
        return vt_ref[0, j, hh * V_DIM:(hh + 1) * V_DIM, :]

    for hh in range(2):
        s = jnp.where(diag_mask, scores_t(hh, i), -jnp.inf)
        m = jnp.max(s, axis=0, keepdims=True)
        p = jnp.exp2(s - m)
        m_ref[hh] = m
        l_ref[hh] = jnp.sum(p, axis=0, keepdims=True)
        acc_ref[hh] = _dot(values_t(hh, i), p.astype(BF16))

    def step(j, carry):
        for hh in range(2):
            s = scores_t(hh, j)
            m_old = m_ref[hh]
            m_new = jnp.maximum(m_old, jnp.max(s, axis=0, keepdims=True))
            alpha = jnp.exp2(m_old - m_new)
            p = jnp.exp2(s - m_new)
            l_ref[hh] = alpha * l_ref[hh] + jnp.sum(p, axis=0, keepdims=True)
            acc_ref[hh] = alpha * acc_ref[hh] + _dot(values_t(hh, j), p.astype(BF16))
            m_ref[hh] = m_new
        return carry

    lax.fori_loop(0, i, step, 0)

    o_t = jnp.concatenate([acc_ref[0] / l_ref[0], acc_ref[1] / l_ref[1]], axis=0)
    o_ref[0] = o_t.T.astype(BF16)


def _attn_bounded_body(q_ref, k_ref, vt_ref, o_ref):
    n_tiles = q_ref.shape[2] // TQ
    key_chunk = lax.broadcasted_iota(jnp.int32, (TK, TQ), 0) // CHUNK
    qry_chunk = lax.broadcasted_iota(jnp.int32, (TK, TQ), 1) // CHUNK
    diag_mask = key_chunk <= qry_chunk

    for i in range(n_tiles):
        heads = []
        for hh in range(2):
            q = q_ref[0, hh, i * TQ:(i + 1) * TQ, :]
            l = None
            acc = None
            for j in range(i + 1):
                kb = k_ref[0, hh, j * TK:(j + 1) * TK, :]
                p = jnp.exp2(lax.dot_general(kb, q, (((1,), (1,)), ((), ())),
                                             preferred_element_type=F32))
                if j == i:
                    p = jnp.where(diag_mask, p, 0.0)
                ps = jnp.sum(p.reshape(TK // SUBLANES, SUBLANES, TQ), axis=0)
                pv = _dot(vt_ref[0, j, hh * V_DIM:(hh + 1) * V_DIM, :], p.astype(BF16))
                l = ps if l is None else l + ps
                acc = pv if acc is None else acc + pv
            heads.append(acc / jnp.sum(l, axis=0, keepdims=True))
        o_t = jnp.concatenate(heads, axis=0)
        o_ref[0, i * TQ:(i + 1) * TQ, :] = o_t.T.astype(BF16)


def _attention_bounded(q, k, vt):
    b, _, s, _ = q.shape
    return pl.pallas_call(
        _attn_bounded_body,
        name="attn_bounded",
        grid=(b, N_HEADS // 2),
        in_specs=[pl.BlockSpec((1, 2, s, HEAD_PAD), lambda bi, p: (bi, p, 0, 0)),
                  pl.BlockSpec((1, 2, s, HEAD_PAD), lambda bi, p: (bi, p, 0, 0)),
                  pl.BlockSpec((1, s // TK, 2 * V_DIM, TK), lambda bi, p: (bi, 0, p, 0))],
        out_specs=pl.BlockSpec((1, s, 2 * V_DIM), lambda bi, p: (bi, 0, p)),
        out_shape=jax.ShapeDtypeStruct((b, s, N_HEADS * V_DIM), BF16),
        compiler_params=pltpu.CompilerParams(
            dimension_semantics=("arbitrary", "arbitrary"),
            vmem_limit_bytes=VMEM_LIMIT),
    )(q, k, vt)


def _attention(q, k, vt):
    b, _, s, _ = q.shape
    assert TQ == TK
    scratch = [pltpu.VMEM((2, 1, TQ), F32), pltpu.VMEM((2, 1, TQ), F32),
               pltpu.VMEM((2, V_DIM, TQ), F32)]
    return pl.pallas_call(
        _attn_body,
        name="attn",
        grid=(b, N_HEADS // 2, s // TQ),
        in_specs=[pl.BlockSpec((1, 2, TQ, HEAD_PAD), lambda bi, p, i: (bi, p, i, 0)),
                  pl.BlockSpec((1, 2, s, HEAD_PAD), lambda bi, p, i: (bi, p, 0, 0)),
                  pl.BlockSpec((1, s // TK, 2 * V_DIM, TK), lambda bi, p, i: (bi, 0, p, 0))],
        out_specs=pl.BlockSpec((1, TQ, 2 * V_DIM), lambda bi, p, i: (bi, i, p)),
        out_shape=jax.ShapeDtypeStruct((b, s, N_HEADS * V_DIM), BF16),
        scratch_shapes=scratch,
        compiler_params=pltpu.CompilerParams(
            dimension_semantics=("arbitrary", "arbitrary", "arbitrary"),
            vmem_limit_bytes=VMEM_LIMIT),
    )(q, k, vt)


def _mix_body(x_ref, u_ref, halo_ref, o_ref, mixg_ref, wg_ref, gbias_ref,
              cw_ref, cb_ref, lng_ref, lnb_ref, wco_ref, wmo_ref, wout_ref,
              out_ref, ubuf_ref):
    tm = TM_MIX
    i = pl.program_id(1)
    x = x_ref[0]
    h = _rms(x, mixg_ref[...]).astype(BF16)
    gates = jax.nn.sigmoid(_dot(h, wg_ref[...]) + gbias_ref[...])

    halo = halo_ref[0]
    ubuf_ref[0:HALO, :] = jnp.where(i > 0, halo, jnp.zeros_like(halo))
    ubuf_ref[HALO:HALO + tm, :] = u_ref[0]
    first = HALO - (CONV_WIDTH - 1)
    conv = None
    for kk in range(CONV_WIDTH):
        tap = cw_ref[kk:kk + 1, :] * ubuf_ref[first + kk:first + kk + tm, :]
        conv = tap if conv is None else conv + tap
    conv = conv + cb_ref[...]
    mu = jnp.mean(conv, axis=-1, keepdims=True)
    cc = conv - mu
    ln = cc * lax.rsqrt(jnp.mean(cc * cc, axis=-1, keepdims=True) + EPS)
    ln = ln * lng_ref[...] + lnb_ref[...]
    act = (ln * jax.nn.sigmoid(ln)).astype(BF16)
    y_conv = _dot(act, wco_ref[...])
    y_mla = _dot(o_ref[0], wmo_ref[...])
    y = gates[:, :D_MODEL] * y_conv + gates[:, D_MODEL:] * y_mla
    out_ref[0] = x + _dot(y.astype(BF16), wout_ref[...])


def _mix(x3, u, o, mixg, w_gates, gbias, cw, cb, lng, lnb, wco, wmo, wout):
    b, s, _ = x3.shape
    tm = TM_MIX
    tok = lambda w: pl.BlockSpec((1, tm, w), lambda bi, i: (bi, i, 0))
    halo = pl.BlockSpec(
        (1, HALO, C_CONV), lambda bi, i: (bi, jnp.maximum(i * (tm // HALO) - 1, 0), 0))
    consts = [mixg, w_gates, gbias, cw, cb, lng, lnb, wco, wmo, wout]
    return pl.pallas_call(
        _mix_body,
        name="mix",
        grid=(b, s // tm),
        in_specs=[tok(D_MODEL), tok(C_CONV), halo, tok(N_HEADS * V_DIM)]
                 + [_const_spec(c.shape) for c in consts],
        out_specs=tok(D_MODEL),
        out_shape=jax.ShapeDtypeStruct((b, s, D_MODEL), F32),
        scratch_shapes=[pltpu.VMEM((HALO + tm, C_CONV), F32)],
        compiler_params=pltpu.CompilerParams(
            dimension_semantics=("arbitrary", "arbitrary"),
            vmem_limit_bytes=VMEM_LIMIT),
    )(x3, u, u, o, *consts)


def kernel(x, positions, ffn1_norm, ffn1_w_gate, ffn1_w_up, ffn1_w_down, mix_norm, w_in,
           gate_bias, conv_w, conv_b, conv_ln_g, conv_ln_b, w_conv_out, cq_norm, ckv_norm,
           w_uq, w_ukv, q_norm, k_norm, w_mla_out, w_out, ffn2_norm, ffn2_w_gate, ffn2_w_up,
           ffn2_w_down):
    b, s, d = x.shape
    depth = ffn1_norm.shape[0]
    assert d == D_MODEL and s % max(TM_PROJ, TM_MIX, TQ) == 0 and (b * s) % TM_FFN == 0
    row = lambda v: v.reshape(1, -1)
    cos_a, sin_b = _rope_tables(positions)
    for l in range(depth):
        wgu, wd = _prep_ffn_weights(ffn1_w_gate[l], ffn1_w_up[l], ffn1_w_down[l])
        x = _ffn(x.reshape(b * s, d), row(ffn1_norm[l]), wgu, wd).reshape(b, s, d)

        wa, w_gates, wq, wk, wvt, ga, gb, gka, gkb, q_off, k_off, bounded = _prep_mixer_weights(
            w_in[l], w_uq[l], w_ukv[l], q_norm[l], k_norm[l])
        mixg = row(mix_norm[l])
        u, q, k, vt = _inproj(x, mixg, wa, row(cq_norm[l]), row(ckv_norm[l]), wq, wk, wvt,
                              ga, gb, gka, gkb, q_off, k_off, cos_a, sin_b)
        o = lax.cond(bounded, _attention_bounded, _attention, q, k, vt)
        x = _mix(x, u, o, mixg, w_gates, gate_bias[l].reshape(1, 2 * D_MODEL),
                 conv_w[l], row(conv_b[l]), row(conv_ln_g[l]), row(conv_ln_b[l]),
                 w_conv_out[l].astype(BF16), w_mla_out[l].astype(BF16),
                 w_out[l].astype(BF16))

        wgu, wd = _prep_ffn_weights(ffn2_w_gate[l], ffn2_w_up[l], ffn2_w_down[l])
        x = _ffn(x.reshape(b * s, d), row(ffn2_norm[l]), wgu, wd).reshape(b, s, d)
    return x
```

```python
import functools
import math

import jax
import jax.numpy as jnp
from jax import lax
from jax.experimental import pallas as pl
from jax.experimental.pallas import tpu as pltpu

D_MODEL = 1024
D_FF = 2816
C_CONV = 512
CONV_WIDTH = 31
N_HEADS = 8
NOPE_DIM = 64
ROPE_DIM = 32
QK_DIM = NOPE_DIM + ROPE_DIM
V_DIM = 64
Q_LORA = 256
KV_LORA = 256
CHUNK = 64
ROPE_THETA = 10000.0
EPS = 1e-6

LANES = 128
SUBLANES = 8
MAX_FIXED_OFFSET = 50.0
HEAD_PAD = LANES
HALF_ROPE = ROPE_DIM // 2
VMEM_LIMIT = 56 * 1024 * 1024

FF_CHUNK = 256
N_FF_CHUNKS = D_FF // FF_CHUNK
TM_FFN = 512
TM_PROJ = 512
TM_MIX = 512
TQ = 512
TK = TM_PROJ
HALO = 32
SHIFT_ROWS = TM_MIX + HALO - SUBLANES
CONV_ROWS = 64

F32 = jnp.float32
BF16 = jnp.bfloat16


def _const_spec(shape):
    zeros = (0,) * len(shape)
    return pl.BlockSpec(shape, lambda *_: zeros, pipeline_mode=pl.Buffered(1))


def _rms(x, g):
    ms = jnp.mean(x * x, axis=-1, keepdims=True)
    return x * lax.rsqrt(ms + EPS) * g


def _dot(a, b):
    return jnp.dot(a, b, preferred_element_type=F32)


def _rope_body(pos_ref, inv_ref, cos_ref, sin_ref):
    ang = pos_ref[...].astype(F32) * inv_ref[...]
    cos_ref[...] = jnp.cos(ang)
    sin_ref[...] = jnp.sin(ang)


def _rope_tables(positions):
    b, s = positions.shape
    t = b * s
    rows = t * HALF_ROPE // LANES
    inv_freq = ROPE_THETA ** (-jnp.arange(0, ROPE_DIM, 2, dtype=F32) / ROPE_DIM)
    pos_rep = jnp.repeat(positions.reshape(t), HALF_ROPE).reshape(rows, LANES)
    inv_rep = jnp.tile(inv_freq, LANES // HALF_ROPE).reshape(1, LANES)
    tr = 512
    cos, sin = pl.pallas_call(
        _rope_body,
        name="rope_tables",
        grid=(rows // tr,),
        in_specs=[pl.BlockSpec((tr, LANES), lambda i: (i, 0)),
                  pl.BlockSpec((1, LANES), lambda i: (0, 0))],
        out_specs=[pl.BlockSpec((tr, LANES), lambda i: (i, 0))] * 2,
        out_shape=[jax.ShapeDtypeStruct((rows, LANES), F32)] * 2,
    )(pos_rep, inv_rep)
    cos = cos.reshape(b, s, HALF_ROPE)
    sin = sin.reshape(b, s, HALF_ROPE)
    ones = jnp.ones((b, s, NOPE_DIM), F32)
    zeros_lo = jnp.zeros((b, s, NOPE_DIM), F32)
    zeros_hi = jnp.zeros((b, s, HEAD_PAD - QK_DIM), F32)
    cos_a = jnp.concatenate([ones, cos, cos, zeros_hi], axis=-1)
    sin_b = jnp.concatenate([zeros_lo, -sin, sin, zeros_hi], axis=-1)
    return cos_a, sin_b


def _ffn_value(x, g, wgu_ref, wd_ref):
    xn = _rms(x, g).astype(BF16)
    acc = None
    for c in range(N_FF_CHUNKS):
        gu = _dot(xn, wgu_ref[c])
        gate = gu[:, :FF_CHUNK]
        up = gu[:, FF_CHUNK:]
        a = (gate * jax.nn.sigmoid(gate) * up).astype(BF16)
        d = _dot(a, wd_ref[c])
        acc = d if acc is None else acc + d
    return x + 0.5 * acc


def _ffn_body(x_ref, g_ref, wgu_ref, wd_ref, o_ref):
    o_ref[...] = _ffn_value(x_ref[...], g_ref[...], wgu_ref, wd_ref)


def _ffn(x2, g, wgu, wd):
    t = x2.shape[0]
    return pl.pallas_call(
        _ffn_body,
        name="ffn",
        grid=(t // TM_FFN,),
        in_specs=[pl.BlockSpec((TM_FFN, D_MODEL), lambda i: (i, 0)),
                  _const_spec((1, D_MODEL)),
                  _const_spec((N_FF_CHUNKS, D_MODEL, 2 * FF_CHUNK)),
                  _const_spec((N_FF_CHUNKS, FF_CHUNK, D_MODEL))],
        out_specs=pl.BlockSpec((TM_FFN, D_MODEL), lambda i: (i, 0)),
        out_shape=jax.ShapeDtypeStruct((t, D_MODEL), F32),
        compiler_params=pltpu.CompilerParams(
            dimension_semantics=("arbitrary",), vmem_limit_bytes=VMEM_LIMIT),
    )(x2, g, wgu, wd)


def _prep_ffn_weights(w_gate, w_up, w_down):
    wg = w_gate.astype(BF16).reshape(D_MODEL, N_FF_CHUNKS, FF_CHUNK)
    wu = w_up.astype(BF16).reshape(D_MODEL, N_FF_CHUNKS, FF_CHUNK)
    wgu = jnp.concatenate([wg, wu], axis=-1).transpose(1, 0, 2)
    wd = w_down.astype(BF16).reshape(N_FF_CHUNKS, FF_CHUNK, D_MODEL)
    return wgu, wd


def _inproj_body(x_ref, mixg_ref, wa_ref, cqg_ref, ckvg_ref, wq_ref, wk_ref, wvt_ref,
                 ga_ref, gb_ref, gka_ref, gkb_ref, qoff_ref, koff_ref, ca_ref, sb_ref,
                 u_ref, q_ref, k_ref, vt_ref):
    h = _rms(x_ref[0], mixg_ref[...]).astype(BF16)
    proj = _dot(h, wa_ref[...])
    o1 = 2 * C_CONV
    o2 = o1 + Q_LORA
    o3 = o2 + KV_LORA
    u_ref[0] = proj[:, :C_CONV] * jax.nn.sigmoid(proj[:, C_CONV:o1])
    cqn = _rms(proj[:, o1:o2], cqg_ref[...]).astype(BF16)
    ckvn = _rms(proj[:, o2:o3], ckvg_ref[...]).astype(BF16)
    kra = proj[:, o3:o3 + HEAD_PAD]
    krb = proj[:, o3 + HEAD_PAD:o3 + 2 * HEAD_PAD]
    qq = _dot(cqn, wq_ref[...])
    kv = _dot(ckvn, wk_ref[...])
    vt_ref[0, 0] = lax.dot_general(wvt_ref[...], ckvn, (((1,), (1,)), ((), ())),
                                   preferred_element_type=F32).astype(BF16)
    ca = ca_ref[0]
    sb = sb_ref[0]
    q_cos = ga_ref[...] * ca
    q_sin = gb_ref[...] * sb
    gka = gka_ref[...]
    k_rot = kra * (gka * ca) + krb * (gkb_ref[...] * sb)
    ss_rope = jnp.sum(kra * kra, axis=-1, keepdims=True)
    hw = N_HEADS * HEAD_PAD
    for hh in range(N_HEADS):
        lo = hh * HEAD_PAD
        qa = qq[:, lo:lo + HEAD_PAD]
        qb = qq[:, hw + lo:hw + lo + HEAD_PAD]
        rq = lax.rsqrt(jnp.sum(qa * qa, axis=-1, keepdims=True) / QK_DIM + EPS)
        q_ref[0, hh] = (rq * (qa * q_cos + qb * q_sin) + qoff_ref[...]).astype(BF16)
        kn = kv[:, lo:lo + HEAD_PAD]
        ss = jnp.sum(kn * kn, axis=-1, keepdims=True) + ss_rope
        rk = lax.rsqrt(ss / QK_DIM + EPS)
        k_ref[0, hh] = (rk * (kn * gka + k_rot) + koff_ref[...]).astype(BF16)


def _inproj(x3, mixg, wa, cqg, ckvg, wq, wk, wvt, ga, gb, gka, gkb, q_off, k_off,
            cos_a, sin_b):
    b, s, _ = x3.shape
    tm = TM_PROJ
    hv = N_HEADS * V_DIM
    tok = lambda w: pl.BlockSpec((1, tm, w), lambda bi, i: (bi, i, 0))
    head = pl.BlockSpec((1, N_HEADS, tm, HEAD_PAD), lambda bi, i: (bi, 0, i, 0))
    vt_spec = pl.BlockSpec((1, 1, hv, tm), lambda bi, i: (bi, i, 0, 0))
    return pl.pallas_call(
        _inproj_body,
        name="inproj",
        grid=(b, s // tm),
        in_specs=[tok(D_MODEL), _const_spec(mixg.shape), _const_spec(wa.shape),
                  _const_spec(cqg.shape), _const_spec(ckvg.shape),
                  _const_spec(wq.shape), _const_spec(wk.shape), _const_spec(wvt.shape),
                  _const_spec(ga.shape), _const_spec(gb.shape),
                  _const_spec(gka.shape), _const_spec(gkb.shape),
                  _const_spec(q_off.shape), _const_spec(k_off.shape),
                  tok(HEAD_PAD), tok(HEAD_PAD)],
        out_specs=[tok(C_CONV), head, head, vt_spec],
        out_shape=[jax.ShapeDtypeStruct((b, s, C_CONV), F32),
                   jax.ShapeDtypeStruct((b, N_HEADS, s, HEAD_PAD), BF16),
                   jax.ShapeDtypeStruct((b, N_HEADS, s, HEAD_PAD), BF16),
                   jax.ShapeDtypeStruct((b, s // tm, hv, tm), BF16)],
        compiler_params=pltpu.CompilerParams(
            dimension_semantics=("arbitrary", "arbitrary"),
            vmem_limit_bytes=VMEM_LIMIT),
    )(x3, mixg, wa, cqg, ckvg, wq, wk, wvt, ga, gb, gka, gkb, q_off, k_off, cos_a, sin_b)


def _pad_lanes(w, lo, total=HEAD_PAD):
    pad = [(0, 0)] * (w.ndim - 1) + [(lo, total - lo - w.shape[-1])]
    return jnp.pad(w, pad)


def _swap_halves(w):
    return jnp.concatenate([w[..., HALF_ROPE:], w[..., :HALF_ROPE]], axis=-1)


def _prep_mixer_weights(w_in, w_uq, w_ukv, q_norm, k_norm):
    o1 = 2 * C_CONV
    o3 = o1 + Q_LORA + KV_LORA
    o4 = o3 + ROPE_DIM
    w_rope = w_in[:, o3:o4]
    wa = jnp.concatenate(
        [w_in[:, :o3], _pad_lanes(w_rope, NOPE_DIM),
         _pad_lanes(_swap_halves(w_rope), NOPE_DIM)], axis=-1).astype(BF16)
    w_gates = w_in[:, o4:].astype(BF16)

    wq3 = w_uq.reshape(Q_LORA, N_HEADS, QK_DIM)
    wq_a = _pad_lanes(wq3, 0).reshape(Q_LORA, N_HEADS * HEAD_PAD)
    wq_b = _pad_lanes(_swap_halves(wq3[..., NOPE_DIM:]), NOPE_DIM)
    wq_b = wq_b.reshape(Q_LORA, N_HEADS * HEAD_PAD)
    wq = jnp.concatenate([wq_a, wq_b], axis=-1).astype(BF16)

    wkv3 = w_ukv.reshape(KV_LORA, N_HEADS, NOPE_DIM + V_DIM)
    wk = _pad_lanes(wkv3[..., :NOPE_DIM], 0).reshape(KV_LORA, N_HEADS * HEAD_PAD)
    wk = wk.astype(BF16)
    wvt = wkv3[..., NOPE_DIM:].reshape(KV_LORA, N_HEADS * V_DIM).T.astype(BF16)

    q_scale = QK_DIM ** -0.5 * math.log2(math.e)
    ga = _pad_lanes(q_norm * q_scale, 0).reshape(1, HEAD_PAD)
    gb = _pad_lanes(_swap_halves(q_norm[NOPE_DIM:]) * q_scale, NOPE_DIM).reshape(1, HEAD_PAD)
    gka = _pad_lanes(k_norm, 0).reshape(1, HEAD_PAD)
    gkb = _pad_lanes(_swap_halves(k_norm[NOPE_DIM:]), NOPE_DIM).reshape(1, HEAD_PAD)

    bound = QK_DIM * q_scale * jnp.max(jnp.abs(q_norm)) * jnp.max(jnp.abs(k_norm))
    bounded = bound <= MAX_FIXED_OFFSET
    lane = jnp.arange(HEAD_PAD) == QK_DIM
    q_off = jnp.where(lane & bounded, -bound, 0.0).astype(F32).reshape(1, HEAD_PAD)
    k_off = jnp.where(lane, 1.0, 0.0).astype(F32).reshape(1, HEAD_PAD)
    return wa, w_gates, wq, wk, wvt, ga, gb, gka, gkb, q_off, k_off, bounded


def _attn_body(q_ref, k_ref, vt_ref, o_ref, m_ref, l_ref, acc_ref):
    i = pl.program_id(2)
    key_chunk = lax.broadcasted_iota(jnp.int32, (TK, TQ), 0) // CHUNK
    qry_chunk = lax.broadcasted_iota(jnp.int32, (TK, TQ), 1) // CHUNK
    diag_mask = key_chunk <= qry_chunk

    def scores_t(hh, j):
        kb = k_ref[0, hh, pl.ds(pl.multiple_of(j * TK, TK), TK), :]
        return lax.dot_general(kb, q_ref[0, hh], (((1,), (1,)), ((), ())),
                               preferred_element_type=F32)

    def values_t(hh, j):
        return vt_ref[0, j, hh * V_DIM:(hh + 1) * V_DIM, :]

    for hh in range(2):
        s = jnp.where(diag_mask, scores_t(hh, i), -jnp.inf)
        m = jnp.max(s, axis=0, keepdims=True)
        p = jnp.exp2(s - m)
        m_ref[hh] = m
        l_ref[hh] = jnp.sum(p, axis=0, keepdims=True)
        acc_ref[hh] = _dot(values_t(hh, i), p.astype(BF16))

    def step(j, carry):
        for hh in range(2):
            s = scores_t(hh, j)
            m_old = m_ref[hh]
            m_new = jnp.maximum(m_old, jnp.max(s, axis=0, keepdims=True))
            alpha = jnp.exp2(m_old - m_new)
            p = jnp.exp2(s - m_new)
            l_ref[hh] = alpha * l_ref[hh] + jnp.sum(p, axis=0, keepdims=True)
            acc_ref[hh] = alpha * acc_ref[hh] + _dot(values_t(hh, j), p.astype(BF16))
            m_ref[hh] = m_new
        return carry

    lax.fori_loop(0, i, step, 0)

    o_t = jnp.concatenate([acc_ref[0] / l_ref[0], acc_ref[1] / l_ref[1]], axis=0)
    o_ref[0] = o_t.T.astype(BF16)


def _attn_bounded_body(q_ref, k_ref, vt_ref, o_ref):
    n_tiles = q_ref.shape[2] // TQ
    key_chunk = lax.broadcasted_iota(jnp.int32, (TK, TQ), 0) // CHUNK
    qry_chunk = lax.broadcasted_iota(jnp.int32, (TK, TQ), 1) // CHUNK
    diag_mask = key_chunk <= qry_chunk

    for i in range(n_tiles):
        heads = []
        for hh in range(2):
            q = q_ref[0, hh, i * TQ:(i + 1) * TQ, :]
            l = None
            acc = None
            for j in range(i + 1):
                kb = k_ref[0, hh, j * TK:(j + 1) * TK, :]
                p = jnp.exp2(lax.dot_general(kb, q, (((1,), (1,)), ((), ())),
                                             preferred_element_type=F32))
                if j == i:
                    p = jnp.where(diag_mask, p, 0.0)
                ps = jnp.sum(p.reshape(TK // SUBLANES, SUBLANES, TQ), axis=0)
                pv = _dot(vt_ref[0, j, hh * V_DIM:(hh + 1) * V_DIM, :], p.astype(BF16))
                l = ps if l is None else l + ps
                acc = pv if acc is None else acc + pv
            heads.append(acc / jnp.sum(l, axis=0, keepdims=True))
        o_t = jnp.concatenate(heads, axis=0)
        o_ref[0, i * TQ:(i + 1) * TQ, :] = o_t.T.astype(BF16)


def _attention_bounded(q, k, vt):
    b, _, s, _ = q.shape
    return pl.pallas_call(
        _attn_bounded_body,
        name="attn_bounded",
        grid=(b, N_HEADS // 2),
        in_specs=[pl.BlockSpec((1, 2, s, HEAD_PAD), lambda bi, p: (bi, p, 0, 0)),
                  pl.BlockSpec((1, 2, s, HEAD_PAD), lambda bi, p: (bi, p, 0, 0)),
                  pl.BlockSpec((1, s // TK, 2 * V_DIM, TK), lambda bi, p: (bi, 0, p, 0))],
        out_specs=pl.BlockSpec((1, s, 2 * V_DIM), lambda bi, p: (bi, 0, p)),
        out_shape=jax.ShapeDtypeStruct((b, s, N_HEADS * V_DIM), BF16),
        compiler_params=pltpu.CompilerParams(
            dimension_semantics=("arbitrary", "arbitrary"),
            vmem_limit_bytes=VMEM_LIMIT),
    )(q, k, vt)


def _attention(q, k, vt):
    b, _, s, _ = q.shape
    assert TQ == TK
    scratch = [pltpu.VMEM((2, 1, TQ), F32), pltpu.VMEM((2, 1, TQ), F32),
               pltpu.VMEM((2, V_DIM, TQ), F32)]
    return pl.pallas_call(
        _attn_body,
        name="attn",
        grid=(b, N_HEADS // 2, s // TQ),
        in_specs=[pl.BlockSpec((1, 2, TQ, HEAD_PAD), lambda bi, p, i: (bi, p, i, 0)),
                  pl.BlockSpec((1, 2, s, HEAD_PAD), lambda bi, p, i: (bi, p, 0, 0)),
                  pl.BlockSpec((1, s // TK, 2 * V_DIM, TK), lambda bi, p, i: (bi, 0, p, 0))],
        out_specs=pl.BlockSpec((1, TQ, 2 * V_DIM), lambda bi, p, i: (bi, i, p)),
        out_shape=jax.ShapeDtypeStruct((b, s, N_HEADS * V_DIM), BF16),
        scratch_shapes=scratch,
        compiler_params=pltpu.CompilerParams(
            dimension_semantics=("arbitrary", "arbitrary", "arbitrary"),
            vmem_limit_bytes=VMEM_LIMIT),
    )(q, k, vt)


def _mix_body(x_ref, u_ref, halo_ref, o_ref, mixg_ref, wg_ref, gbias_ref,
              cw_ref, cb_ref, lng_ref, lnb_ref, wco_ref, wmo_ref, wout_ref,
              out_ref, ubuf_ref, shift_ref, conv_ref):
    tm = TM_MIX
    i = pl.program_id(1)
    x = x_ref[0]
    h = _rms(x, mixg_ref[...]).astype(BF16)
    gates = jax.nn.sigmoid(_dot(h, wg_ref[...]) + gbias_ref[...])

    halo = halo_ref[0]
    ubuf_ref[0:HALO, :] = jnp.where(i > 0, halo, jnp.zeros_like(halo))
    ubuf_ref[HALO:HALO + tm, :] = u_ref[0]
    span = SHIFT_ROWS
    for bb in range(1, SUBLANES):
        shift_ref[bb - 1] = ubuf_ref[bb:bb + span, :]
    first = HALO - (CONV_WIDTH - 1)
    for cb in range(C_CONV // LANES):
        cols = slice(cb * LANES, (cb + 1) * LANES)
        taps = [cw_ref[kk:kk + 1, cols] for kk in range(CONV_WIDTH)]
        bias = cb_ref[:, cols]
        for rb in range(tm // CONV_ROWS):
            acc = None
            for kk in range(CONV_WIDTH):
                aa, bb = divmod(first + kk, SUBLANES)
                r0 = aa * SUBLANES + rb * CONV_ROWS
                if bb == 0:
                    win = ubuf_ref[r0:r0 + CONV_ROWS, cols]
                else:
                    win = shift_ref[bb - 1, r0:r0 + CONV_ROWS, cols]
                term = taps[kk] * win
                acc = term if acc is None else acc + term
            conv_ref[rb * CONV_ROWS:(rb + 1) * CONV_ROWS, cols] = acc + bias
    conv = conv_ref[...]
    mu = jnp.mean(conv, axis=-1, keepdims=True)
    cc = conv - mu
    ln = cc * lax.rsqrt(jnp.mean(cc * cc, axis=-1, keepdims=True) + EPS)
    ln = ln * lng_ref[...] + lnb_ref[...]
    act = (ln * jax.nn.sigmoid(ln)).astype(BF16)
    y_conv = _dot(act, wco_ref[...])
    y_mla = _dot(o_ref[0], wmo_ref[...])
    y = gates[:, :D_MODEL] * y_conv + gates[:, D_MODEL:] * y_mla
    out_ref[0] = x + _dot(y.astype(BF16), wout_ref[...])


def _mix(x3, u, o, mixg, w_gates, gbias, cw, cb, lng, lnb, wco, wmo, wout):
    b, s, _ = x3.shape
    tm = TM_MIX
    tok = lambda w: pl.BlockSpec((1, tm, w), lambda bi, i: (bi, i, 0))
    halo = pl.BlockSpec(
        (1, HALO, C_CONV), lambda bi, i: (bi, jnp.maximum(i * (tm // HALO) - 1, 0), 0))
    consts = [mixg, w_gates, gbias, cw, cb, lng, lnb, wco, wmo, wout]
    return pl.pallas_call(
        _mix_body,
        name="mix",
        grid=(b, s // tm),
        in_specs=[tok(D_MODEL), tok(C_CONV), halo, tok(N_HEADS * V_DIM)]
                 + [_const_spec(c.shape) for c in consts],
        out_specs=tok(D_MODEL),
        out_shape=jax.ShapeDtypeStruct((b, s, D_MODEL), F32),
        scratch_shapes=[pltpu.VMEM((HALO + tm, C_CONV), F32),
                        pltpu.VMEM((SUBLANES - 1, SHIFT_ROWS, C_CONV), F32),
                        pltpu.VMEM((tm, C_CONV), F32)],
        compiler_params=pltpu.CompilerParams(
            dimension_semantics=("arbitrary", "arbitrary"),
            vmem_limit_bytes=VMEM_LIMIT),
    )(x3, u, u, o, *consts)


def kernel(x, positions, ffn1_norm, ffn1_w_gate, ffn1_w_up, ffn1_w_down, mix_norm, w_in,
           gate_bias, conv_w, conv_b, conv_ln_g, conv_ln_b, w_conv_out, cq_norm, ckv_norm,
           w_uq, w_ukv, q_norm, k_norm, w_mla_out, w_out, ffn2_norm, ffn2_w_gate, ffn2_w_up,
           ffn2_w_down):
    b, s, d = x.shape
    depth = ffn1_norm.shape[0]
    assert d == D_MODEL and s % max(TM_PROJ, TM_MIX, TQ) == 0 and (b * s) % TM_FFN == 0
    row = lambda v: v.reshape(1, -1)
    cos_a, sin_b = _rope_tables(positions)
    for l in range(depth):
        wgu, wd = _prep_ffn_weights(ffn1_w_gate[l], ffn1_w_up[l], ffn1_w_down[l])
        x = _ffn(x.reshape(b * s, d), row(ffn1_norm[l]), wgu, wd).reshape(b, s, d)

        wa, w_gates, wq, wk, wvt, ga, gb, gka, gkb, q_off, k_off, bounded = _prep_mixer_weights(
            w_in[l], w_uq[l], w_ukv[l], q_norm[l], k_norm[l])
        mixg = row(mix_norm[l])
        u, q, k, vt = _inproj(x, mixg, wa, row(cq_norm[l]), row(ckv_norm[l]), wq, wk, wvt,
                              ga, gb, gka, gkb, q_off, k_off, cos_a, sin_b)
        o = lax.cond(bounded, _attention_bounded, _attention, q, k, vt)
        x = _mix(x, u, o, mixg, w_gates, gate_bias[l].reshape(1, 2 * D_MODEL),
                 conv_w[l], row(conv_b[l]), row(conv_ln_g[l]), row(conv_ln_b[l]),
                 w_conv_out[l].astype(BF16), w_mla_out[l].astype(BF16),
                 w_out[l].astype(BF16))

        wgu, wd = _prep_ffn_weights(ffn2_w_gate[l], ffn2_w_up[l], ffn2_w_down[l])
        x = _ffn(x.reshape(b * s, d), row(ffn2_norm[l]), wgu, wd).reshape(b, s, d)
    return x
```

```python
import functools
import math

import jax
import jax.numpy as jnp
from jax import lax
from jax.experimental import pallas as pl
from jax.experimental.pallas import tpu as pltpu

D_MODEL = 1024
D_FF = 2816
C_CONV = 512
CONV_WIDTH = 31
N_HEADS = 8
NOPE_DIM = 64
ROPE_DIM = 32
QK_DIM = NOPE_DIM + ROPE_DIM
V_DIM = 64
Q_LORA = 256
KV_LORA = 256
CHUNK = 64
ROPE_THETA = 10000.0
EPS = 1e-6

LANES = 128
SUBLANES = 8
MAX_FIXED_OFFSET = 50.0
HEAD_PAD = LANES
HALF_ROPE = ROPE_DIM // 2
VMEM_LIMIT = 56 * 1024 * 1024

FF_CHUNK = 256
N_FF_CHUNKS = D_FF // FF_CHUNK
TM_FFN = 1024
TM_PROJ = 512
TM_MIX = 512
TQ = 512
TK = TM_PROJ
HALO = 32
SHIFT_ROWS = TM_MIX + HALO - SUBLANES
CONV_ROWS = 64

F32 = jnp.float32
BF16 = jnp.bfloat16


def _const_spec(shape):
    zeros = (0,) * len(shape)
    return pl.BlockSpec(shape, lambda *_: zeros, pipeline_mode=pl.Buffered(1))


def _rms(x, g):
    ms = jnp.mean(x * x, axis=-1, keepdims=True)
    return x * lax.rsqrt(ms + EPS) * g


def _dot(a, b):
    return jnp.dot(a, b, preferred_element_type=F32)


def _rope_body(pos_ref, inv_ref, cos_ref, sin_ref):
    ang = pos_ref[...].astype(F32) * inv_ref[...]
    cos_ref[...] = jnp.cos(ang)
    sin_ref[...] = jnp.sin(ang)


def _rope_tables(positions):
    b, s = positions.shape
    t = b * s
    rows = t * HALF_ROPE // LANES
    inv_freq = ROPE_THETA ** (-jnp.arange(0, ROPE_DIM, 2, dtype=F32) / ROPE_DIM)
    pos_rep = jnp.repeat(positions.reshape(t), HALF_ROPE).reshape(rows, LANES)
    inv_rep = jnp.tile(inv_freq, LANES // HALF_ROPE).reshape(1, LANES)
    tr = 512
    cos, sin = pl.pallas_call(
        _rope_body,
        name="rope_tables",
        grid=(rows // tr,),
        in_specs=[pl.BlockSpec((tr, LANES), lambda i: (i, 0)),
                  pl.BlockSpec((1, LANES), lambda i: (0, 0))],
        out_specs=[pl.BlockSpec((tr, LANES), lambda i: (i, 0))] * 2,
        out_shape=[jax.ShapeDtypeStruct((rows, LANES), F32)] * 2,
    )(pos_rep, inv_rep)
    cos = cos.reshape(b, s, HALF_ROPE)
    sin = sin.reshape(b, s, HALF_ROPE)
    ones = jnp.ones((b, s, NOPE_DIM), F32)
    zeros_lo = jnp.zeros((b, s, NOPE_DIM), F32)
    zeros_hi = jnp.zeros((b, s, HEAD_PAD - QK_DIM), F32)
    cos_a = jnp.concatenate([ones, cos, cos, zeros_hi], axis=-1)
    sin_b = jnp.concatenate([zeros_lo, -sin, sin, zeros_hi], axis=-1)
    return cos_a, sin_b


def _ffn_value(x, g, wg_ref, wu_ref, wd_ref):
    xn = _rms(x, g).astype(BF16)
    acc = None
    for c in range(N_FF_CHUNKS):
        cols = slice(c * FF_CHUNK, (c + 1) * FF_CHUNK)
        gate = _dot(xn, wg_ref[:, cols])
        up = _dot(xn, wu_ref[:, cols])
        a = (gate * jax.nn.sigmoid(gate) * up).astype(BF16)
        d = _dot(a, wd_ref[cols, :])
        acc = d if acc is None else acc + d
    return x + 0.5 * acc


def _ffn_body(x_ref, g_ref, wg_ref, wu_ref, wd_ref, o_ref):
    o_ref[...] = _ffn_value(x_ref[...], g_ref[...], wg_ref, wu_ref, wd_ref)


def _ffn(x2, g, w_gate, w_up, w_down):
    t = x2.shape[0]
    return pl.pallas_call(
        _ffn_body,
        name="ffn",
        grid=(t // TM_FFN,),
        in_specs=[pl.BlockSpec((TM_FFN, D_MODEL), lambda i: (i, 0)),
                  _const_spec((1, D_MODEL)),
                  _const_spec((D_MODEL, D_FF)),
                  _const_spec((D_MODEL, D_FF)),
                  _const_spec((D_FF, D_MODEL))],
        out_specs=pl.BlockSpec((TM_FFN, D_MODEL), lambda i: (i, 0)),
        out_shape=jax.ShapeDtypeStruct((t, D_MODEL), F32),
        compiler_params=pltpu.CompilerParams(
            dimension_semantics=("arbitrary",), vmem_limit_bytes=VMEM_LIMIT),
    )(x2, g, w_gate.astype(BF16), w_up.astype(BF16), w_down.astype(BF16))


def _inproj_body(x_ref, mixg_ref, wa_ref, cqg_ref, ckvg_ref, wq_ref, wk_ref, wvt_ref,
                 ga_ref, gb_ref, gka_ref, gkb_ref, qoff_ref, koff_ref, ca_ref, sb_ref,
                 h_ref, u_ref, q_ref, k_ref, vt_ref):
    h = _rms(x_ref[0], mixg_ref[...]).astype(BF16)
    h_ref[0] = h
    proj = _dot(h, wa_ref[...])
    o1 = 2 * C_CONV
    o2 = o1 + Q_LORA
    o3 = o2 + KV_LORA
    u_ref[0] = proj[:, :C_CONV] * jax.nn.sigmoid(proj[:, C_CONV:o1])
    cqn = _rms(proj[:, o1:o2], cqg_ref[...]).astype(BF16)
    ckvn = _rms(proj[:, o2:o3], ckvg_ref[...]).astype(BF16)
    kra = proj[:, o3:o3 + HEAD_PAD]
    krb = proj[:, o3 + HEAD_PAD:o3 + 2 * HEAD_PAD]
    qq = _dot(cqn, wq_ref[...])
    kv = _dot(ckvn, wk_ref[...])
    vt_ref[0, 0] = lax.dot_general(wvt_ref[...], ckvn, (((1,), (1,)), ((), ())),
                                   preferred_element_type=F32).astype(BF16)
    ca = ca_ref[0]
    sb = sb_ref[0]
    q_cos = ga_ref[...] * ca
    q_sin = gb_ref[...] * sb
    gka = gka_ref[...]
    k_rot = kra * (gka * ca) + krb * (gkb_ref[...] * sb)
    ss_rope = jnp.sum(kra * kra, axis=-1, keepdims=True)
    hw = N_HEADS * HEAD_PAD
    for hh in range(N_HEADS):
        lo = hh * HEAD_PAD
        qa = qq[:, lo:lo + HEAD_PAD]
        qb = qq[:, hw + lo:hw + lo + HEAD_PAD]
        rq = lax.rsqrt(jnp.sum(qa * qa, axis=-1, keepdims=True) / QK_DIM + EPS)
        q_ref[0, hh] = (rq * (qa * q_cos + qb * q_sin) + qoff_ref[...]).astype(BF16)
        kn = kv[:, lo:lo + HEAD_PAD]
        ss = jnp.sum(kn * kn, axis=-1, keepdims=True) + ss_rope
        rk = lax.rsqrt(ss / QK_DIM + EPS)
        k_ref[0, hh] = (rk * (kn * gka + k_rot) + koff_ref[...]).astype(BF16)


def _inproj(x3, mixg, wa, cqg, ckvg, wq, wk, wvt, ga, gb, gka, gkb, q_off, k_off,
            cos_a, sin_b):
    b, s, _ = x3.shape
    tm = TM_PROJ
    hv = N_HEADS * V_DIM
    tok = lambda w: pl.BlockSpec((1, tm, w), lambda bi, i: (bi, i, 0))
    head = pl.BlockSpec((1, N_HEADS, tm, HEAD_PAD), lambda bi, i: (bi, 0, i, 0))
    vt_spec = pl.BlockSpec((1, 1, hv, tm), lambda bi, i: (bi, i, 0, 0))
    return pl.pallas_call(
        _inproj_body,
        name="inproj",
        grid=(b, s // tm),
        in_specs=[tok(D_MODEL), _const_spec(mixg.shape), _const_spec(wa.shape),
                  _const_spec(cqg.shape), _const_spec(ckvg.shape),
                  _const_spec(wq.shape), _const_spec(wk.shape), _const_spec(wvt.shape),
                  _const_spec(ga.shape), _const_spec(gb.shape),
                  _const_spec(gka.shape), _const_spec(gkb.shape),
                  _const_spec(q_off.shape), _const_spec(k_off.shape),
                  tok(HEAD_PAD), tok(HEAD_PAD)],
        out_specs=[tok(D_MODEL), tok(C_CONV), head, head, vt_spec],
        out_shape=[jax.ShapeDtypeStruct((b, s, D_MODEL), BF16),
                   jax.ShapeDtypeStruct((b, s, C_CONV), F32),
                   jax.ShapeDtypeStruct((b, N_HEADS, s, HEAD_PAD), BF16),
                   jax.ShapeDtypeStruct((b, N_HEADS, s, HEAD_PAD), BF16),
                   jax.ShapeDtypeStruct((b, s // tm, hv, tm), BF16)],
        compiler_params=pltpu.CompilerParams(
            dimension_semantics=("arbitrary", "arbitrary"),
            vmem_limit_bytes=VMEM_LIMIT),
    )(x3, mixg, wa, cqg, ckvg, wq, wk, wvt, ga, gb, gka, gkb, q_off, k_off, cos_a, sin_b)


def _pad_lanes(w, lo, total=HEAD_PAD):
    pad = [(0, 0)] * (w.ndim - 1) + [(lo, total - lo - w.shape[-1])]
    return jnp.pad(w, pad)


def _swap_halves(w):
    return jnp.concatenate([w[..., HALF_ROPE:], w[..., :HALF_ROPE]], axis=-1)


def _prep_mixer_weights(w_in, w_uq, w_ukv, q_norm, k_norm):
    o1 = 2 * C_CONV
    o3 = o1 + Q_LORA + KV_LORA
    o4 = o3 + ROPE_DIM
    w_rope = w_in[:, o3:o4]
    wa = jnp.concatenate(
        [w_in[:, :o3], _pad_lanes(w_rope, NOPE_DIM),
         _pad_lanes(_swap_halves(w_rope), NOPE_DIM)], axis=-1).astype(BF16)
    w_gates = w_in[:, o4:].astype(BF16)

    wq3 = w_uq.reshape(Q_LORA, N_HEADS, QK_DIM)
    wq_a = _pad_lanes(wq3, 0).reshape(Q_LORA, N_HEADS * HEAD_PAD)
    wq_b = _pad_lanes(_swap_halves(wq3[..., NOPE_DIM:]), NOPE_DIM)
    wq_b = wq_b.reshape(Q_LORA, N_HEADS * HEAD_PAD)
    wq = jnp.concatenate([wq_a, wq_b], axis=-1).astype(BF16)

    wkv3 = w_ukv.reshape(KV_LORA, N_HEADS, NOPE_DIM + V_DIM)
    wk = _pad_lanes(wkv3[..., :NOPE_DIM], 0).reshape(KV_LORA, N_HEADS * HEAD_PAD)
    wk = wk.astype(BF16)
    wvt = wkv3[..., NOPE_DIM:].reshape(KV_LORA, N_HEADS * V_DIM).T.astype(BF16)

    q_scale = QK_DIM ** -0.5 * math.log2(math.e)
    ga = _pad_lanes(q_norm * q_scale, 0).reshape(1, HEAD_PAD)
    gb = _pad_lanes(_swap_halves(q_norm[NOPE_DIM:]) * q_scale, NOPE_DIM).reshape(1, HEAD_PAD)
    gka = _pad_lanes(k_norm, 0).reshape(1, HEAD_PAD)
    gkb = _pad_lanes(_swap_halves(k_norm[NOPE_DIM:]), NOPE_DIM).reshape(1, HEAD_PAD)

    bound = QK_DIM * q_scale * jnp.max(jnp.abs(q_norm)) * jnp.max(jnp.abs(k_norm))
    bounded = bound <= MAX_FIXED_OFFSET
    lane = jnp.arange(HEAD_PAD) == QK_DIM
    q_off = jnp.where(lane & bounded, -bound, 0.0).astype(F32).reshape(1, HEAD_PAD)
    k_off = jnp.where(lane, 1.0, 0.0).astype(F32).reshape(1, HEAD_PAD)
    return wa, w_gates, wq, wk, wvt, ga, gb, gka, gkb, q_off, k_off, bounded


def _attn_body(q_ref, k_ref, vt_ref, o_ref, m_ref, l_ref, acc_ref):
    i = pl.program_id(2)
    key_chunk = lax.broadcasted_iota(jnp.int32, (TK, TQ), 0) // CHUNK
    qry_chunk = lax.broadcasted_iota(jnp.int32, (TK, TQ), 1) // CHUNK
    diag_mask = key_chunk <= qry_chunk

    def scores_t(hh, j):
        kb = k_ref[0, hh, pl.ds(pl.multiple_of(j * TK, TK), TK), :]
        return lax.dot_general(kb, q_ref[0, hh], (((1,), (1,)), ((), ())),
                               preferred_element_type=F32)

    def values_t(hh, j):
        return vt_ref[0, j, hh * V_DIM:(hh + 1) * V_DIM, :]

    for hh in range(2):
        s = jnp.where(diag_mask, scores_t(hh, i), -jnp.inf)
        m = jnp.max(s, axis=0, keepdims=True)
        p = jnp.exp2(s - m)
        m_ref[hh] = m
        l_ref[hh] = jnp.sum(p, axis=0, keepdims=True)
        acc_ref[hh] = _dot(values_t(hh, i), p.astype(BF16))

    def step(j, carry):
        for hh in range(2):
            s = scores_t(hh, j)
            m_old = m_ref[hh]
            m_new = jnp.maximum(m_old, jnp.max(s, axis=0, keepdims=True))
            alpha = jnp.exp2(m_old - m_new)
            p = jnp.exp2(s - m_new)
            l_ref[hh] = alpha * l_ref[hh] + jnp.sum(p, axis=0, keepdims=True)
            acc_ref[hh] = alpha * acc_ref[hh] + _dot(values_t(hh, j), p.astype(BF16))
            m_ref[hh] = m_new
        return carry

    lax.fori_loop(0, i, step, 0)

    o_t = jnp.concatenate([acc_ref[0] / l_ref[0], acc_ref[1] / l_ref[1]], axis=0)
    o_ref[0] = o_t.T.astype(BF16)


def _attn_bounded_body(q_ref, k_ref, vt_ref, o_ref):
    n_tiles = q_ref.shape[2] // TQ
    key_chunk = lax.broadcasted_iota(jnp.int32, (TK, TQ), 0) // CHUNK
    qry_chunk = lax.broadcasted_iota(jnp.int32, (TK, TQ), 1) // CHUNK
    diag_mask = key_chunk <= qry_chunk

    for i in range(n_tiles):
        heads = []
        for hh in range(2):
            q = q_ref[0, hh, i * TQ:(i + 1) * TQ, :]
            l = None
            acc = None
            for j in range(i + 1):
                kb = k_ref[0, hh, j * TK:(j + 1) * TK, :]
                p = jnp.exp2(lax.dot_general(kb, q, (((1,), (1,)), ((), ())),
                                             preferred_element_type=F32))
                if j == i:
                    p = jnp.where(diag_mask, p, 0.0)
                ps = jnp.sum(p.reshape(TK // SUBLANES, SUBLANES, TQ), axis=0)
                pv = _dot(vt_ref[0, j, hh * V_DIM:(hh + 1) * V_DIM, :], p.astype(BF16))
                l = ps if l is None else l + ps
                acc = pv if acc is None else acc + pv
            heads.append(acc / jnp.sum(l, axis=0, keepdims=True))
        o_t = jnp.concatenate(heads, axis=0)
        o_ref[0, i * TQ:(i + 1) * TQ, :] = o_t.T.astype(BF16)


def _attention_bounded(q, k, vt):
    b, _, s, _ = q.shape
    return pl.pallas_call(
        _attn_bounded_body,
        name="attn_bounded",
        grid=(b, N_HEADS // 2),
        in_specs=[pl.BlockSpec((1, 2, s, HEAD_PAD), lambda bi, p: (bi, p, 0, 0)),
                  pl.BlockSpec((1, 2, s, HEAD_PAD), lambda bi, p: (bi, p, 0, 0)),
                  pl.BlockSpec((1, s // TK, 2 * V_DIM, TK), lambda bi, p: (bi, 0, p, 0))],
        out_specs=pl.BlockSpec((1, s, 2 * V_DIM), lambda bi, p: (bi, 0, p)),
        out_shape=jax.ShapeDtypeStruct((b, s, N_HEADS * V_DIM), BF16),
        compiler_params=pltpu.CompilerParams(
            dimension_semantics=("arbitrary", "arbitrary"),
            vmem_limit_bytes=VMEM_LIMIT),
    )(q, k, vt)


def _attention(q, k, vt):
    b, _, s, _ = q.shape
    assert TQ == TK
    scratch = [pltpu.VMEM((2, 1, TQ), F32), pltpu.VMEM((2, 1, TQ), F32),
               pltpu.VMEM((2, V_DIM, TQ), F32)]
    return pl.pallas_call(
        _attn_body,
        name="attn",
        grid=(b, N_HEADS // 2, s // TQ),
        in_specs=[pl.BlockSpec((1, 2, TQ, HEAD_PAD), lambda bi, p, i: (bi, p, i, 0)),
                  pl.BlockSpec((1, 2, s, HEAD_PAD), lambda bi, p, i: (bi, p, 0, 0)),
                  pl.BlockSpec((1, s // TK, 2 * V_DIM, TK), lambda bi, p, i: (bi, 0, p, 0))],
        out_specs=pl.BlockSpec((1, TQ, 2 * V_DIM), lambda bi, p, i: (bi, i, p)),
        out_shape=jax.ShapeDtypeStruct((b, s, N_HEADS * V_DIM), BF16),
        scratch_shapes=scratch,
        compiler_params=pltpu.CompilerParams(
            dimension_semantics=("arbitrary", "arbitrary", "arbitrary"),
            vmem_limit_bytes=VMEM_LIMIT),
    )(q, k, vt)


def _conv_branch(u, halo, cw_ref, cb_ref, lng_ref, lnb_ref, ubuf_ref, shift_ref, conv_ref):
    tm = TM_MIX
    ubuf_ref[0:HALO, :] = halo
    ubuf_ref[HALO:HALO + tm, :] = u
    for bb in range(1, SUBLANES):
        shift_ref[bb - 1] = ubuf_ref[bb:bb + SHIFT_ROWS, :]
    first = HALO - (CONV_WIDTH - 1)
    for cb in range(C_CONV // LANES):
        cols = slice(cb * LANES, (cb + 1) * LANES)
        for rb in range(tm // CONV_ROWS):
            acc = None
            for kk in range(CONV_WIDTH):
                aa, bb = divmod(first + kk, SUBLANES)
                r0 = aa * SUBLANES + rb * CONV_ROWS
                if bb == 0:
                    win = ubuf_ref[r0:r0 + CONV_ROWS, cols]
                else:
                    win = shift_ref[bb - 1, r0:r0 + CONV_ROWS, cols]
                term = cw_ref[kk:kk + 1, cols] * win
                acc = term if acc is None else acc + term
            conv_ref[rb * CONV_ROWS:(rb + 1) * CONV_ROWS, cols] = acc + cb_ref[:, cols]
    conv = conv_ref[...]
    mu = jnp.mean(conv, axis=-1, keepdims=True)
    cc = conv - mu
    ln = cc * lax.rsqrt(jnp.mean(cc * cc, axis=-1, keepdims=True) + EPS)
    ln = ln * lng_ref[...] + lnb_ref[...]
    return (ln * jax.nn.sigmoid(ln)).astype(BF16)


def _mix_body(x_ref, h_ref, u_ref, halo_ref, o_ref, wg_ref, gbias_ref,
              cw_ref, cb_ref, lng_ref, lnb_ref, wco_ref, wmo_ref, wout_ref,
              out_ref, ubuf_ref, shift_ref, conv_ref):
    i = pl.program_id(1)
    gates = jax.nn.sigmoid(_dot(h_ref[0], wg_ref[...]) + gbias_ref[...])
    halo = halo_ref[0]
    halo = jnp.where(i > 0, halo, jnp.zeros_like(halo))
    act = _conv_branch(u_ref[0], halo, cw_ref, cb_ref, lng_ref, lnb_ref,
                       ubuf_ref, shift_ref, conv_ref)
    y_conv = _dot(act, wco_ref[...])
    y_mla = _dot(o_ref[0], wmo_ref[...])
    y = gates[:, :D_MODEL] * y_conv + gates[:, D_MODEL:] * y_mla
    out_ref[0] = x_ref[0] + _dot(y.astype(BF16), wout_ref[...])


def _mix(x3, h, u, o, w_gates, gbias, cw, cb, lng, lnb, wco, wmo, wout):
    b, s, _ = x3.shape
    tm = TM_MIX
    tok = lambda w: pl.BlockSpec((1, tm, w), lambda bi, i: (bi, i, 0))
    halo = pl.BlockSpec(
        (1, HALO, C_CONV), lambda bi, i: (bi, jnp.maximum(i * (tm // HALO) - 1, 0), 0))
    consts = [w_gates, gbias, cw, cb, lng, lnb, wco, wmo, wout]
    return pl.pallas_call(
        _mix_body,
        name="mix",
        grid=(b, s // tm),
        in_specs=[tok(D_MODEL), tok(D_MODEL), tok(C_CONV), halo, tok(N_HEADS * V_DIM)]
                 + [_const_spec(c.shape) for c in consts],
        out_specs=tok(D_MODEL),
        out_shape=jax.ShapeDtypeStruct((b, s, D_MODEL), F32),
        scratch_shapes=[pltpu.VMEM((HALO + tm, C_CONV), F32),
                        pltpu.VMEM((SUBLANES - 1, SHIFT_ROWS, C_CONV), F32),
                        pltpu.VMEM((tm, C_CONV), F32)],
        compiler_params=pltpu.CompilerParams(
            dimension_semantics=("arbitrary", "arbitrary"),
            vmem_limit_bytes=VMEM_LIMIT),
    )(x3, h, u, u, o, *consts)


def kernel(x, positions, ffn1_norm, ffn1_w_gate, ffn1_w_up, ffn1_w_down, mix_norm, w_in,
           gate_bias, conv_w, conv_b, conv_ln_g, conv_ln_b, w_conv_out, cq_norm, ckv_norm,
           w_uq, w_ukv, q_norm, k_norm, w_mla_out, w_out, ffn2_norm, ffn2_w_gate, ffn2_w_up,
           ffn2_w_down):
    b, s, d = x.shape
    depth = ffn1_norm.shape[0]
    assert d == D_MODEL and s % max(TM_PROJ, TM_MIX, TQ) == 0 and (b * s) % TM_FFN == 0
    row = lambda v: v.reshape(1, -1)
    cos_a, sin_b = _rope_tables(positions)
    for l in range(depth):
        x = _ffn(x.reshape(b * s, d), row(ffn1_norm[l]), ffn1_w_gate[l], ffn1_w_up[l],
                 ffn1_w_down[l]).reshape(b, s, d)

        wa, w_gates, wq, wk, wvt, ga, gb, gka, gkb, q_off, k_off, bounded = _prep_mixer_weights(
            w_in[l], w_uq[l], w_ukv[l], q_norm[l], k_norm[l])
        mixg = row(mix_norm[l])
        h, u, q, k, vt = _inproj(x, mixg, wa, row(cq_norm[l]), row(ckv_norm[l]), wq, wk, wvt,
                                 ga, gb, gka, gkb, q_off, k_off, cos_a, sin_b)
        o = lax.cond(bounded, _attention_bounded, _attention, q, k, vt)
        x = _mix(x, h, u, o, w_gates, gate_bias[l].reshape(1, 2 * D_MODEL),
                 conv_w[l], row(conv_b[l]), row(conv_ln_g[l]), row(conv_ln_b[l]),
                 w_conv_out[l].astype(BF16), w_mla_out[l].astype(BF16),
                 w_out[l].astype(BF16))

        x = _ffn(x.reshape(b * s, d), row(ffn2_norm[l]), ffn2_w_gate[l], ffn2_w_up[l],
                 ffn2_w_down[l]).reshape(b, s, d)
    return x
```

```python
import functools
import math

import jax
import jax.numpy as jnp
from jax import lax
from jax.experimental import pallas as pl
from jax.experimental.pallas import tpu as pltpu

D_MODEL = 1024
D_FF = 2816
C_CONV = 512
CONV_WIDTH = 31
N_HEADS = 8
NOPE_DIM = 64
ROPE_DIM = 32
QK_DIM = NOPE_DIM + ROPE_DIM
V_DIM = 64
Q_LORA = 256
KV_LORA = 256
CHUNK = 64
ROPE_THETA = 10000.0
EPS = 1e-6

LANES = 128
SUBLANES = 8
MAX_FIXED_OFFSET = 50.0
HEAD_PAD = LANES
HALF_ROPE = ROPE_DIM // 2
VMEM_LIMIT = 56 * 1024 * 1024
FF_CHUNK = 256
N_FF_CHUNKS = D_FF // FF_CHUNK
TM_FFN = 1024
TM_PROJ = 512
TM_MIX = 512
TQ = 512
TK = TM_PROJ
ATTN_BLOCK = 512
HALO = 32
SHIFT_ROWS = TM_MIX + HALO - SUBLANES
CONV_ROWS = 64

F32 = jnp.float32
BF16 = jnp.bfloat16


def _const_spec(shape):
    zeros = (0,) * len(shape)
    return pl.BlockSpec(shape, lambda *_: zeros, pipeline_mode=pl.Buffered(1))


def _rms(x, g):
    ms = jnp.mean(x * x, axis=-1, keepdims=True)
    return x * lax.rsqrt(ms + EPS) * g


def _dot(a, b):
    return jnp.dot(a, b, preferred_element_type=F32)


def _rope_body(pos_ref, inv_ref, cos_ref, sin_ref, nsin_ref):
    ang = pos_ref[...].astype(F32) * inv_ref[...]
    sin = jnp.sin(ang)
    cos_ref[...] = jnp.cos(ang)
    sin_ref[...] = sin
    nsin_ref[...] = -sin


def _rope_tables(positions):
    b, s = positions.shape
    t = b * s
    rows = t * HALF_ROPE // LANES
    inv_freq = ROPE_THETA ** (-jnp.arange(0, ROPE_DIM, 2, dtype=F32) / ROPE_DIM)
    pos_rep = jnp.broadcast_to(positions.reshape(t, 1), (t, HALF_ROPE)).reshape(rows, LANES)
    inv_rep = jnp.tile(inv_freq, LANES // HALF_ROPE).reshape(1, LANES)
    tr = 512
    cos, sin, nsin = pl.pallas_call(
        _rope_body,
        name="rope_tables",
        grid=(rows // tr,),
        in_specs=[pl.BlockSpec((tr, LANES), lambda i: (i, 0)),
                  pl.BlockSpec((1, LANES), lambda i: (0, 0))],
        out_specs=[pl.BlockSpec((tr, LANES), lambda i: (i, 0))] * 3,
        out_shape=[jax.ShapeDtypeStruct((rows, LANES), F32)] * 3,
    )(pos_rep, inv_rep)
    cos = cos.reshape(b, s, HALF_ROPE)
    sin = sin.reshape(b, s, HALF_ROPE)
    nsin = nsin.reshape(b, s, HALF_ROPE)
    ones = jnp.ones((b, s, NOPE_DIM), F32)
    zeros_lo = jnp.zeros((b, s, NOPE_DIM), F32)
    zeros_hi = jnp.zeros((b, s, HEAD_PAD - QK_DIM), F32)
    cos_a = jnp.concatenate([ones, cos, cos, zeros_hi], axis=-1)
    sin_b = jnp.concatenate([zeros_lo, nsin, sin, zeros_hi], axis=-1)
    return cos_a, sin_b


def _ffn_value(x, g, wg_ref, wu_ref, wd_ref):
    xn = _rms(x, g).astype(BF16)
    acc = None
    for c in range(N_FF_CHUNKS):
        cols = slice(c * FF_CHUNK, (c + 1) * FF_CHUNK)
        gate = _dot(xn, wg_ref[0, :, cols])
        up = _dot(xn, wu_ref[0, :, cols])
        a = (gate * jax.nn.sigmoid(gate) * up).astype(BF16)
        d = _dot(a, wd_ref[0, cols, :])
        acc = d if acc is None else acc + d
    return x + 0.5 * acc


def _ffn_body(x_ref, g_ref, wg_ref, wu_ref, wd_ref, o_ref):
    o_ref[...] = _ffn_value(x_ref[...], g_ref[...], wg_ref, wu_ref, wd_ref)


def _ffn(x2, g, w_gate, w_up, w_down, layer):
    t = x2.shape[0]
    tok = pl.BlockSpec((TM_FFN, D_MODEL), lambda i: (i, 0))
    layer_spec = lambda w: pl.BlockSpec((1,) + w.shape[1:], lambda i: (layer, 0, 0),
                                        pipeline_mode=pl.Buffered(1))
    return pl.pallas_call(
        _ffn_body,
        name="ffn",
        grid=(t // TM_FFN,),
        in_specs=[tok, _const_spec((1, D_MODEL)),
                  layer_spec(w_gate), layer_spec(w_up), layer_spec(w_down)],
        out_specs=tok,
        out_shape=jax.ShapeDtypeStruct((t, D_MODEL), F32),
        compiler_params=pltpu.CompilerParams(
            dimension_semantics=("arbitrary",), vmem_limit_bytes=VMEM_LIMIT),
    )(x2, g, w_gate, w_up, w_down)


def _inproj_body(x_ref, mixg_ref, wa_ref, cqg_ref, ckvg_ref, wq_ref, wk_ref, wvt_ref,
                 ga_ref, gb_ref, gka_ref, gkb_ref, qoff_ref, koff_ref, ca_ref, sb_ref,
                 h_ref, u_ref, q_ref, k_ref, vt_ref):
    h = _rms(x_ref[0], mixg_ref[...]).astype(BF16)
    h_ref[0] = h
    proj = _dot(h, wa_ref[...])
    o1 = 2 * C_CONV
    o2 = o1 + Q_LORA
    o3 = o2 + KV_LORA
    u_ref[0] = proj[:, :C_CONV] * jax.nn.sigmoid(proj[:, C_CONV:o1])
    cqn = _rms(proj[:, o1:o2], cqg_ref[...]).astype(BF16)
    ckvn = _rms(proj[:, o2:o3], ckvg_ref[...]).astype(BF16)
    kra = proj[:, o3:o3 + HEAD_PAD]
    krb = proj[:, o3 + HEAD_PAD:o3 + 2 * HEAD_PAD]
    qq = _dot(cqn, wq_ref[...])
    kv = _dot(ckvn, wk_ref[...])
    vt_ref[0, 0] = lax.dot_general(wvt_ref[...], ckvn, (((1,), (1,)), ((), ())),
                                   preferred_element_type=F32).astype(BF16)
    ca = ca_ref[0]
    sb = sb_ref[0]
    q_cos = ga_ref[...] * ca
    q_sin = gb_ref[...] * sb
    gka = gka_ref[...]
    k_rot = kra * (gka * ca) + krb * (gkb_ref[...] * sb)
    ss_rope = jnp.sum(kra * kra, axis=-1, keepdims=True)
    hw = N_HEADS * HEAD_PAD
    for hh in range(N_HEADS):
        lo = hh * HEAD_PAD
        qa = qq[:, lo:lo + HEAD_PAD]
        qb = qq[:, hw + lo:hw + lo + HEAD_PAD]
        rq = lax.rsqrt(jnp.sum(qa * qa, axis=-1, keepdims=True) / QK_DIM + EPS)
        q_ref[0, hh] = (rq * (qa * q_cos + qb * q_sin) + qoff_ref[...]).astype(BF16)
        kn = kv[:, lo:lo + HEAD_PAD]
        ss = jnp.sum(kn * kn, axis=-1, keepdims=True) + ss_rope
        rk = lax.rsqrt(ss / QK_DIM + EPS)
        k_ref[0, hh] = (rk * (kn * gka + k_rot) + koff_ref[...]).astype(BF16)


def _inproj(x3, mixg, wa, cqg, ckvg, wq, wk, wvt, ga, gb, gka, gkb, q_off, k_off,
            cos_a, sin_b):
    b, s, _ = x3.shape
    tm = TM_PROJ
    hv = N_HEADS * V_DIM
    tok = lambda w: pl.BlockSpec((1, tm, w), lambda bi, i: (bi, i, 0))
    head = pl.BlockSpec((1, N_HEADS, tm, HEAD_PAD), lambda bi, i: (bi, 0, i, 0))
    vt_spec = pl.BlockSpec((1, 1, hv, tm), lambda bi, i: (bi, i, 0, 0))
    return pl.pallas_call(
        _inproj_body,
        name="inproj",
        grid=(b, s // tm),
        in_specs=[tok(D_MODEL), _const_spec(mixg.shape), _const_spec(wa.shape),
                  _const_spec(cqg.shape), _const_spec(ckvg.shape),
                  _const_spec(wq.shape), _const_spec(wk.shape), _const_spec(wvt.shape),
                  _const_spec(ga.shape), _const_spec(gb.shape),
                  _const_spec(gka.shape), _const_spec(gkb.shape),
                  _const_spec(q_off.shape), _const_spec(k_off.shape),
                  tok(HEAD_PAD), tok(HEAD_PAD)],
        out_specs=[tok(D_MODEL), tok(C_CONV), head, head, vt_spec],
        out_shape=[jax.ShapeDtypeStruct((b, s, D_MODEL), BF16),
                   jax.ShapeDtypeStruct((b, s, C_CONV), F32),
                   jax.ShapeDtypeStruct((b, N_HEADS, s, HEAD_PAD), BF16),
                   jax.ShapeDtypeStruct((b, N_HEADS, s, HEAD_PAD), BF16),
                   jax.ShapeDtypeStruct((b, s // tm, hv, tm), BF16)],
        compiler_params=pltpu.CompilerParams(
            dimension_semantics=("arbitrary", "arbitrary"),
            vmem_limit_bytes=VMEM_LIMIT),
    )(x3, mixg, wa, cqg, ckvg, wq, wk, wvt, ga, gb, gka, gkb, q_off, k_off, cos_a, sin_b)


def _pad_lanes(w, lo, total=HEAD_PAD):
    pad = [(0, 0)] * (w.ndim - 1) + [(lo, total - lo - w.shape[-1])]
    return jnp.pad(w, pad)


def _swap_halves(w):
    return jnp.concatenate([w[..., HALF_ROPE:], w[..., :HALF_ROPE]], axis=-1)


def _prep_mixer_weights(w_in, w_uq, w_ukv, q_norm, k_norm):
    o1 = 2 * C_CONV
    o3 = o1 + Q_LORA + KV_LORA
    o4 = o3 + ROPE_DIM
    w_rope = w_in[:, o3:o4]
    wa = jnp.concatenate(
        [w_in[:, :o3], _pad_lanes(w_rope, NOPE_DIM),
         _pad_lanes(_swap_halves(w_rope), NOPE_DIM)], axis=-1).astype(BF16)
    w_gates = w_in[:, o4:].astype(BF16)

    wq3 = w_uq.reshape(Q_LORA, N_HEADS, QK_DIM)
    wq_a = _pad_lanes(wq3, 0).reshape(Q_LORA, N_HEADS * HEAD_PAD)
    wq_b = _pad_lanes(_swap_halves(wq3[..., NOPE_DIM:]), NOPE_DIM)
    wq_b = wq_b.reshape(Q_LORA, N_HEADS * HEAD_PAD)
    wq = jnp.concatenate([wq_a, wq_b], axis=-1).astype(BF16)

    wkv3 = w_ukv.reshape(KV_LORA, N_HEADS, NOPE_DIM + V_DIM)
    wk = _pad_lanes(wkv3[..., :NOPE_DIM], 0).reshape(KV_LORA, N_HEADS * HEAD_PAD)
    wk = wk.astype(BF16)
    wvt = wkv3[..., NOPE_DIM:].reshape(KV_LORA, N_HEADS * V_DIM).T.astype(BF16)

    q_scale = QK_DIM ** -0.5 * math.log2(math.e)
    ga = _pad_lanes(q_norm * q_scale, 0).reshape(1, HEAD_PAD)
    gb = _pad_lanes(_swap_halves(q_norm[NOPE_DIM:]) * q_scale, NOPE_DIM).reshape(1, HEAD_PAD)
    gka = _pad_lanes(k_norm, 0).reshape(1, HEAD_PAD)
    gkb = _pad_lanes(_swap_halves(k_norm[NOPE_DIM:]), NOPE_DIM).reshape(1, HEAD_PAD)

    bound = QK_DIM * q_scale * jnp.max(jnp.abs(q_norm)) * jnp.max(jnp.abs(k_norm))
    bounded = bound <= MAX_FIXED_OFFSET
    lane = jnp.arange(HEAD_PAD) == QK_DIM
    q_off = jnp.where(lane & bounded, -bound, 0.0).astype(F32).reshape(1, HEAD_PAD)
    k_off = jnp.where(lane, 1.0, 0.0).astype(F32).reshape(1, HEAD_PAD)
    return wa, w_gates, wq, wk, wvt, ga, gb, gka, gkb, q_off, k_off, bounded


def _attn_body(q_ref, k_ref, vt_ref, o_ref, m_ref, l_ref, acc_ref):
    i = pl.program_id(2)
    key_chunk = lax.broadcasted_iota(jnp.int32, (TK, TQ), 0) // CHUNK
    qry_chunk = lax.broadcasted_iota(jnp.int32, (TK, TQ), 1) // CHUNK
    diag_mask = key_chunk <= qry_chunk

    def scores_t(hh, j):
        kb = k_ref[0, hh, pl.ds(pl.multiple_of(j * TK, TK), TK), :]
        return lax.dot_general(kb, q_ref[0, hh], (((1,), (1,)), ((), ())),
                               preferred_element_type=F32)

    def values_t(hh, j):
        return vt_ref[0, j, hh * V_DIM:(hh + 1) * V_DIM, :]

    for hh in range(2):
        s = jnp.where(diag_mask, scores_t(hh, i), -jnp.inf)
        m = jnp.max(s, axis=0, keepdims=True)
        p = jnp.exp2(s - m)
        m_ref[hh] = m
        l_ref[hh] = jnp.sum(p, axis=0, keepdims=True)
        acc_ref[hh] = _dot(values_t(hh, i), p.astype(BF16))

    def step(j, carry):
        for hh in range(2):
            s = scores_t(hh, j)
            m_old = m_ref[hh]
            m_new = jnp.maximum(m_old, jnp.max(s, axis=0, keepdims=True))
            alpha = jnp.exp2(m_old - m_new)
            p = jnp.exp2(s - m_new)
            l_ref[hh] = alpha * l_ref[hh] + jnp.sum(p, axis=0, keepdims=True)
            acc_ref[hh] = alpha * acc_ref[hh] + _dot(values_t(hh, j), p.astype(BF16))
            m_ref[hh] = m_new
        return carry

    lax.fori_loop(0, i, step, 0)

    o_t = jnp.concatenate([acc_ref[0] / l_ref[0], acc_ref[1] / l_ref[1]], axis=0)
    o_ref[0] = o_t.T.astype(BF16)


def _attn_bounded_body(q_ref, k_ref, vt_ref, o_ref):
    blk = ATTN_BLOCK
    n_tiles = q_ref.shape[2] // blk
    key_chunk = lax.broadcasted_iota(jnp.int32, (blk, blk), 0) // CHUNK
    qry_chunk = lax.broadcasted_iota(jnp.int32, (blk, blk), 1) // CHUNK
    diag_mask = key_chunk <= qry_chunk

    for i in range(n_tiles):
        heads = []
        for hh in range(2):
            q = q_ref[0, hh, i * blk:(i + 1) * blk, :]
            l = None
            acc = None
            for j in range(i + 1):
                kb = k_ref[0, hh, j * blk:(j + 1) * blk, :]
                p = jnp.exp2(lax.dot_general(kb, q, (((1,), (1,)), ((), ())),
                                             preferred_element_type=F32))
                if j == i:
                    p = jnp.where(diag_mask, p, 0.0)
                ps = jnp.sum(p.reshape(blk // SUBLANES, SUBLANES, blk), axis=0)
                tile, off = divmod(j * blk, TK)
                v_t = vt_ref[0, tile, hh * V_DIM:(hh + 1) * V_DIM, off:off + blk]
                pv = _dot(v_t, p.astype(BF16))
                l = ps if l is None else l + ps
                acc = pv if acc is None else acc + pv
            heads.append(acc / jnp.sum(l, axis=0, keepdims=True))
        o_t = jnp.concatenate(heads, axis=0)
        o_ref[0, i * blk:(i + 1) * blk, :] = o_t.T.astype(BF16)


def _attention_bounded(q, k, vt):
    b, _, s, _ = q.shape
    return pl.pallas_call(
        _attn_bounded_body,
        name="attn_bounded",
        grid=(b, N_HEADS // 2),
        in_specs=[pl.BlockSpec((1, 2, s, HEAD_PAD), lambda bi, p: (bi, p, 0, 0)),
                  pl.BlockSpec((1, 2, s, HEAD_PAD), lambda bi, p: (bi, p, 0, 0)),
                  pl.BlockSpec((1, s // TK, 2 * V_DIM, TK), lambda bi, p: (bi, 0, p, 0))],
        out_specs=pl.BlockSpec((1, s, 2 * V_DIM), lambda bi, p: (bi, 0, p)),
        out_shape=jax.ShapeDtypeStruct((b, s, N_HEADS * V_DIM), BF16),
        compiler_params=pltpu.CompilerParams(
            dimension_semantics=("arbitrary", "arbitrary"),
            vmem_limit_bytes=VMEM_LIMIT),
    )(q, k, vt)


def _attention(q, k, vt):
    b, _, s, _ = q.shape
    assert TQ == TK
    scratch = [pltpu.VMEM((2, 1, TQ), F32), pltpu.VMEM((2, 1, TQ), F32),
               pltpu.VMEM((2, V_DIM, TQ), F32)]
    return pl.pallas_call(
        _attn_body,
        name="attn",
        grid=(b, N_HEADS // 2, s // TQ),
        in_specs=[pl.BlockSpec((1, 2, TQ, HEAD_PAD), lambda bi, p, i: (bi, p, i, 0)),
                  pl.BlockSpec((1, 2, s, HEAD_PAD), lambda bi, p, i: (bi, p, 0, 0)),
                  pl.BlockSpec((1, s // TK, 2 * V_DIM, TK), lambda bi, p, i: (bi, 0, p, 0))],
        out_specs=pl.BlockSpec((1, TQ, 2 * V_DIM), lambda bi, p, i: (bi, i, p)),
        out_shape=jax.ShapeDtypeStruct((b, s, N_HEADS * V_DIM), BF16),
        scratch_shapes=scratch,
        compiler_params=pltpu.CompilerParams(
            dimension_semantics=("arbitrary", "arbitrary", "arbitrary"),
            vmem_limit_bytes=VMEM_LIMIT),
    )(q, k, vt)


def _conv_fill(u, halo, ubuf_ref, shift_ref):
    ubuf_ref[0:HALO, :] = halo
    ubuf_ref[HALO:HALO + TM_MIX, :] = u
    for bb in range(1, SUBLANES):
        shift_ref[bb - 1] = ubuf_ref[bb:bb + SHIFT_ROWS, :]


def _conv_branch(row0, nrows, cw_ref, cb_ref, lng_ref, lnb_ref, ubuf_ref, shift_ref, conv_ref):
    first = HALO - (CONV_WIDTH - 1)
    for cb in range(C_CONV // LANES):
        cols = slice(cb * LANES, (cb + 1) * LANES)
        for rb in range(row0 // CONV_ROWS, (row0 + nrows) // CONV_ROWS):
            acc = None
            for kk in range(CONV_WIDTH):
                aa, bb = divmod(first + kk, SUBLANES)
                r0 = aa * SUBLANES + rb * CONV_ROWS
                if bb == 0:
                    win = ubuf_ref[r0:r0 + CONV_ROWS, cols]
                else:
                    win = shift_ref[bb - 1, r0:r0 + CONV_ROWS, cols]
                term = cw_ref[kk:kk + 1, cols] * win
                acc = term if acc is None else acc + term
            conv_ref[rb * CONV_ROWS:(rb + 1) * CONV_ROWS, cols] = acc + cb_ref[:, cols]
    conv = conv_ref[row0:row0 + nrows, :]
    mu = jnp.mean(conv, axis=-1, keepdims=True)
    cc = conv - mu
    ln = cc * lax.rsqrt(jnp.mean(cc * cc, axis=-1, keepdims=True) + EPS)
    ln = ln * lng_ref[...] + lnb_ref[...]
    return (ln * jax.nn.sigmoid(ln)).astype(BF16)


def _mix_body(x_ref, h_ref, u_ref, halo_ref, o_ref, wg_ref, gbias_ref,
              cw_ref, cb_ref, lng_ref, lnb_ref, wco_ref, wmo_ref, wout_ref,
              out_ref, ubuf_ref, shift_ref, conv_ref):
    i = pl.program_id(1)
    halo = halo_ref[0]
    halo = jnp.where(i > 0, halo, jnp.zeros_like(halo))
    _conv_fill(u_ref[0], halo, ubuf_ref, shift_ref)
    act = _conv_branch(0, TM_MIX, cw_ref, cb_ref, lng_ref, lnb_ref,
                       ubuf_ref, shift_ref, conv_ref)
    gates = jax.nn.sigmoid(_dot(h_ref[0], wg_ref[...]) + gbias_ref[...])
    y_conv = _dot(act, wco_ref[...])
    y_mla = _dot(o_ref[0], wmo_ref[...])
    y = gates[:, :D_MODEL] * y_conv + gates[:, D_MODEL:] * y_mla
    out_ref[0] = x_ref[0] + _dot(y.astype(BF16), wout_ref[...])


def _mix(x3, h, u, o, w_gates, gbias, cw, cb, lng, lnb, wco, wmo, wout):
    b, s, _ = x3.shape
    tm = TM_MIX
    tok = lambda w: pl.BlockSpec((1, tm, w), lambda bi, i: (bi, i, 0))
    halo = pl.BlockSpec(
        (1, HALO, C_CONV), lambda bi, i: (bi, jnp.maximum(i * (tm // HALO) - 1, 0), 0))
    consts = [w_gates, gbias, cw, cb, lng, lnb, wco, wmo, wout]
    return pl.pallas_call(
        _mix_body,
        name="mix",
        grid=(b, s // tm),
        in_specs=[tok(D_MODEL), tok(D_MODEL), tok(C_CONV), halo, tok(N_HEADS * V_DIM)]
                 + [_const_spec(c.shape) for c in consts],
        out_specs=tok(D_MODEL),
        out_shape=jax.ShapeDtypeStruct((b, s, D_MODEL), F32),
        scratch_shapes=[pltpu.VMEM((HALO + tm, C_CONV), F32),
                        pltpu.VMEM((SUBLANES - 1, SHIFT_ROWS, C_CONV), F32),
                        pltpu.VMEM((tm, C_CONV), F32)],
        compiler_params=pltpu.CompilerParams(
            dimension_semantics=("arbitrary", "arbitrary"),
            vmem_limit_bytes=VMEM_LIMIT),
    )(x3, h, u, u, o, *consts)


def kernel(x, positions, ffn1_norm, ffn1_w_gate, ffn1_w_up, ffn1_w_down, mix_norm, w_in,
           gate_bias, conv_w, conv_b, conv_ln_g, conv_ln_b, w_conv_out, cq_norm, ckv_norm,
           w_uq, w_ukv, q_norm, k_norm, w_mla_out, w_out, ffn2_norm, ffn2_w_gate, ffn2_w_up,
           ffn2_w_down):
    b, s, d = x.shape
    depth = ffn1_norm.shape[0]
    assert d == D_MODEL and s % max(TM_PROJ, TM_MIX, TQ) == 0 and (b * s) % TM_FFN == 0
    row = lambda v: v.reshape(1, -1)
    cos_a, sin_b = _rope_tables(positions)
    ffn1_w = [w.astype(BF16) for w in (ffn1_w_gate, ffn1_w_up, ffn1_w_down)]
    ffn2_w = [w.astype(BF16) for w in (ffn2_w_gate, ffn2_w_up, ffn2_w_down)]
    for l in range(depth):
        x = _ffn(x.reshape(b * s, d), row(ffn1_norm[l]), *ffn1_w, l).reshape(b, s, d)

        wa, w_gates, wq, wk, wvt, ga, gb, gka, gkb, q_off, k_off, bounded = _prep_mixer_weights(
            w_in[l], w_uq[l], w_ukv[l], q_norm[l], k_norm[l])
        h, u, q, k, vt = _inproj(x, row(mix_norm[l]), wa, row(cq_norm[l]), row(ckv_norm[l]),
                                 wq, wk, wvt, ga, gb, gka, gkb, q_off, k_off, cos_a, sin_b)
        o = lax.cond(bounded, _attention_bounded, _attention, q, k, vt)
        x = _mix(x, h, u, o, w_gates, gate_bias[l].reshape(1, 2 * D_MODEL),
                 conv_w[l], row(conv_b[l]), row(conv_ln_g[l]), row(conv_ln_b[l]),
                 w_conv_out[l].astype(BF16), w_mla_out[l].astype(BF16),
                 w_out[l].astype(BF16))

        x = _ffn(x.reshape(b * s, d), row(ffn2_norm[l]), *ffn2_w, l).reshape(b, s, d)
    return x
```

```python
import functools
import math

import jax
import jax.numpy as jnp
from jax import lax
from jax.experimental import pallas as pl
from jax.experimental.pallas import tpu as pltpu

D_MODEL = 1024
D_FF = 2816
C_CONV = 512
CONV_WIDTH = 31
N_HEADS = 8
NOPE_DIM = 64
ROPE_DIM = 32
QK_DIM = NOPE_DIM + ROPE_DIM
V_DIM = 64
Q_LORA = 256
KV_LORA = 256
CHUNK = 64
ROPE_THETA = 10000.0
EPS = 1e-6

LANES = 128
SUBLANES = 8
MAX_FIXED_OFFSET = 50.0
HEAD_PAD = LANES
HALF_ROPE = ROPE_DIM // 2
VMEM_LIMIT = 56 * 1024 * 1024
FF_CHUNK = 256
N_FF_CHUNKS = D_FF // FF_CHUNK
TM_FFN = 1024
TM_PROJ = 512
TM_MIX = 512
TQ = 512
TK = TM_PROJ
ATTN_Q_BLOCK = 2048
HALO = 32
SHIFT_ROWS = TM_MIX + HALO - SUBLANES
CONV_ROWS = 64
F32 = jnp.float32
BF16 = jnp.bfloat16


def _const_spec(shape):
    zeros = (0,) * len(shape)
    return pl.BlockSpec(shape, lambda *_: zeros, pipeline_mode=pl.Buffered(1))


def _rms(x, g):
    ms = jnp.mean(x * x, axis=-1, keepdims=True)
    return x * lax.rsqrt(ms + EPS) * g


def _dot(a, b):
    return jnp.dot(a, b, preferred_element_type=F32)


def _rope_body(pos_ref, inv_ref, cos_ref, sin_ref, nsin_ref):
    ang = pos_ref[...].astype(F32) * inv_ref[...]
    sin = jnp.sin(ang)
    cos_ref[...] = jnp.cos(ang)
    sin_ref[...] = sin
    nsin_ref[...] = -sin


def _rope_tables(positions):
    b, s = positions.shape
    t = b * s
    rows = t * HALF_ROPE // LANES
    inv_freq = ROPE_THETA ** (-jnp.arange(0, ROPE_DIM, 2, dtype=F32) / ROPE_DIM)
    pos_rep = jnp.broadcast_to(positions.reshape(t, 1), (t, HALF_ROPE)).reshape(rows, LANES)
    inv_rep = jnp.tile(inv_freq, LANES // HALF_ROPE).reshape(1, LANES)
    tr = 512
    cos, sin, nsin = pl.pallas_call(
        _rope_body,
        name="rope_tables",
        grid=(rows // tr,),
        in_specs=[pl.BlockSpec((tr, LANES), lambda i: (i, 0)),
                  pl.BlockSpec((1, LANES), lambda i: (0, 0))],
        out_specs=[pl.BlockSpec((tr, LANES), lambda i: (i, 0))] * 3,
        out_shape=[jax.ShapeDtypeStruct((rows, LANES), F32)] * 3,
    )(pos_rep, inv_rep)
    cos = cos.reshape(b, s, HALF_ROPE)
    sin = sin.reshape(b, s, HALF_ROPE)
    nsin = nsin.reshape(b, s, HALF_ROPE)
    ones = jnp.ones((b, s, NOPE_DIM), F32)
    zeros_lo = jnp.zeros((b, s, NOPE_DIM), F32)
    zeros_hi = jnp.zeros((b, s, HEAD_PAD - QK_DIM), F32)
    cos_a = jnp.concatenate([ones, cos, cos, zeros_hi], axis=-1)
    sin_b = jnp.concatenate([zeros_lo, nsin, sin, zeros_hi], axis=-1)
    return cos_a, sin_b


def _ffn_value(x, g, wg_ref, wu_ref, wd_ref):
    xn = _rms(x, g).astype(BF16)
    acc = None
    for c in range(N_FF_CHUNKS):
        cols = slice(c * FF_CHUNK, (c + 1) * FF_CHUNK)
        gate = _dot(xn, wg_ref[0, :, cols])
        up = _dot(xn, wu_ref[0, :, cols])
        a = (gate * jax.nn.sigmoid(gate) * up).astype(BF16)
        d = _dot(a, wd_ref[0, cols, :])
        acc = d if acc is None else acc + d
    return x + 0.5 * acc


def _ffn_body(x_ref, g_ref, wg_ref, wu_ref, wd_ref, o_ref):
    o_ref[...] = _ffn_value(x_ref[...], g_ref[...], wg_ref, wu_ref, wd_ref)


def _ffn(x2, g, w_gate, w_up, w_down, layer):
    t = x2.shape[0]
    tok = pl.BlockSpec((TM_FFN, D_MODEL), lambda i: (i, 0))
    layer_spec = lambda w: pl.BlockSpec((1,) + w.shape[1:], lambda i: (layer, 0, 0),
                                        pipeline_mode=pl.Buffered(1))
    return pl.pallas_call(
        _ffn_body,
        name="ffn",
        grid=(t // TM_FFN,),
        in_specs=[tok, _const_spec((1, D_MODEL)),
                  layer_spec(w_gate), layer_spec(w_up), layer_spec(w_down)],
        out_specs=tok,
        out_shape=jax.ShapeDtypeStruct((t, D_MODEL), F32),
        compiler_params=pltpu.CompilerParams(
            dimension_semantics=("arbitrary",), vmem_limit_bytes=VMEM_LIMIT),
    )(x2, g, w_gate, w_up, w_down)


def _inproj_body(x_ref, mixg_ref, wa_ref, cqg_ref, ckvg_ref, wq_ref, wk_ref, wvt_ref,
                 ga_ref, gb_ref, gka_ref, gkb_ref, qoff_ref, koff_ref, ca_ref, sb_ref,
                 h_ref, u_ref, q_ref, k_ref, vt_ref):
    h = _rms(x_ref[0], mixg_ref[...]).astype(BF16)
    h_ref[0] = h
    proj = _dot(h, wa_ref[...])
    o1 = 2 * C_CONV
    o2 = o1 + Q_LORA
    o3 = o2 + KV_LORA
    u_ref[0] = proj[:, :C_CONV] * jax.nn.sigmoid(proj[:, C_CONV:o1])
    cqn = _rms(proj[:, o1:o2], cqg_ref[...]).astype(BF16)
    ckvn = _rms(proj[:, o2:o3], ckvg_ref[...]).astype(BF16)
    kra = proj[:, o3:o3 + HEAD_PAD]
    krb = proj[:, o3 + HEAD_PAD:o3 + 2 * HEAD_PAD]
    qq = _dot(cqn, wq_ref[...])
    kv = _dot(ckvn, wk_ref[...])
    vt_ref[0, 0] = lax.dot_general(wvt_ref[...], ckvn, (((1,), (1,)), ((), ())),
                                   preferred_element_type=F32).astype(BF16)
    ca = ca_ref[0]
    sb = sb_ref[0]
    q_cos = ga_ref[...] * ca
    q_sin = gb_ref[...] * sb
    gka = gka_ref[...]
    k_rot = kra * (gka * ca) + krb * (gkb_ref[...] * sb)
    ss_rope = jnp.sum(kra * kra, axis=-1, keepdims=True)
    hw = N_HEADS * HEAD_PAD
    for hh in range(N_HEADS):
        lo = hh * HEAD_PAD
        qa = qq[:, lo:lo + HEAD_PAD]
        qb = qq[:, hw + lo:hw + lo + HEAD_PAD]
        rq = lax.rsqrt(jnp.sum(qa * qa, axis=-1, keepdims=True) / QK_DIM + EPS)
        q_ref[0, hh] = (rq * (qa * q_cos + qb * q_sin) + qoff_ref[...]).astype(BF16)
        kn = kv[:, lo:lo + HEAD_PAD]
        ss = jnp.sum(kn * kn, axis=-1, keepdims=True) + ss_rope
        rk = lax.rsqrt(ss / QK_DIM + EPS)
        k_ref[0, hh] = (rk * (kn * gka + k_rot) + koff_ref[...]).astype(BF16)


def _inproj(x3, mixg, wa, cqg, ckvg, wq, wk, wvt, ga, gb, gka, gkb, q_off, k_off,
            cos_a, sin_b):
    b, s, _ = x3.shape
    tm = TM_PROJ
    hv = N_HEADS * V_DIM
    tok = lambda w: pl.BlockSpec((1, tm, w), lambda bi, i: (bi, i, 0))
    head = pl.BlockSpec((1, N_HEADS, tm, HEAD_PAD), lambda bi, i: (bi, 0, i, 0))
    vt_spec = pl.BlockSpec((1, 1, hv, tm), lambda bi, i: (bi, i, 0, 0))
    return pl.pallas_call(
        _inproj_body,
        name="inproj",
        grid=(b, s // tm),
        in_specs=[tok(D_MODEL), _const_spec(mixg.shape), _const_spec(wa.shape),
                  _const_spec(cqg.shape), _const_spec(ckvg.shape),
                  _const_spec(wq.shape), _const_spec(wk.shape), _const_spec(wvt.shape),
                  _const_spec(ga.shape), _const_spec(gb.shape),
                  _const_spec(gka.shape), _const_spec(gkb.shape),
                  _const_spec(q_off.shape), _const_spec(k_off.shape),
                  tok(HEAD_PAD), tok(HEAD_PAD)],
        out_specs=[tok(D_MODEL), tok(C_CONV), head, head, vt_spec],
        out_shape=[jax.ShapeDtypeStruct((b, s, D_MODEL), BF16),
                   jax.ShapeDtypeStruct((b, s, C_CONV), F32),
                   jax.ShapeDtypeStruct((b, N_HEADS, s, HEAD_PAD), BF16),
                   jax.ShapeDtypeStruct((b, N_HEADS, s, HEAD_PAD), BF16),
                   jax.ShapeDtypeStruct((b, s // tm, hv, tm), BF16)],
        compiler_params=pltpu.CompilerParams(
            dimension_semantics=("arbitrary", "arbitrary"),
            vmem_limit_bytes=VMEM_LIMIT),
    )(x3, mixg, wa, cqg, ckvg, wq, wk, wvt, ga, gb, gka, gkb, q_off, k_off, cos_a, sin_b)


def _pad_lanes(w, lo, total=HEAD_PAD):
    pad = [(0, 0)] * (w.ndim - 1) + [(lo, total - lo - w.shape[-1])]
    return jnp.pad(w, pad)


def _swap_halves(w):
    return jnp.concatenate([w[..., HALF_ROPE:], w[..., :HALF_ROPE]], axis=-1)


def _prep_mixer_weights(w_in, w_uq, w_ukv, q_norm, k_norm):
    o1 = 2 * C_CONV
    o3 = o1 + Q_LORA + KV_LORA
    o4 = o3 + ROPE_DIM
    w_rope = w_in[:, o3:o4]
    wa = jnp.concatenate(
        [w_in[:, :o3], _pad_lanes(w_rope, NOPE_DIM),
         _pad_lanes(_swap_halves(w_rope), NOPE_DIM)], axis=-1).astype(BF16)
    w_gates = w_in[:, o4:].astype(BF16)

    wq3 = w_uq.reshape(Q_LORA, N_HEADS, QK_DIM)
    wq_a = _pad_lanes(wq3, 0).reshape(Q_LORA, N_HEADS * HEAD_PAD)
    wq_b = _pad_lanes(_swap_halves(wq3[..., NOPE_DIM:]), NOPE_DIM)
    wq_b = wq_b.reshape(Q_LORA, N_HEADS * HEAD_PAD)
    wq = jnp.concatenate([wq_a, wq_b], axis=-1).astype(BF16)

    wkv3 = w_ukv.reshape(KV_LORA, N_HEADS, NOPE_DIM + V_DIM)
    wk = _pad_lanes(wkv3[..., :NOPE_DIM], 0).reshape(KV_LORA, N_HEADS * HEAD_PAD)
    wk = wk.astype(BF16)
    wvt = wkv3[..., NOPE_DIM:].reshape(KV_LORA, N_HEADS * V_DIM).T.astype(BF16)

    q_scale = QK_DIM ** -0.5 * math.log2(math.e)
    ga = _pad_lanes(q_norm * q_scale, 0).reshape(1, HEAD_PAD)
    gb = _pad_lanes(_swap_halves(q_norm[NOPE_DIM:]) * q_scale, NOPE_DIM).reshape(1, HEAD_PAD)
    gka = _pad_lanes(k_norm, 0).reshape(1, HEAD_PAD)
    gkb = _pad_lanes(_swap_halves(k_norm[NOPE_DIM:]), NOPE_DIM).reshape(1, HEAD_PAD)

    bound = QK_DIM * q_scale * jnp.max(jnp.abs(q_norm)) * jnp.max(jnp.abs(k_norm))
    bounded = bound <= MAX_FIXED_OFFSET
    lane = jnp.arange(HEAD_PAD) == QK_DIM
    q_off = jnp.where(lane & bounded, -bound, 0.0).astype(F32).reshape(1, HEAD_PAD)
    k_off = jnp.where(lane, 1.0, 0.0).astype(F32).reshape(1, HEAD_PAD)
    return wa, w_gates, wq, wk, wvt, ga, gb, gka, gkb, q_off, k_off, bounded


def _attn_body(q_ref, k_ref, vt_ref, o_ref, m_ref, l_ref, acc_ref):
    i = pl.program_id(2)
    key_chunk = lax.broadcasted_iota(jnp.int32, (TK, TQ), 0) // CHUNK
    qry_chunk = lax.broadcasted_iota(jnp.int32, (TK, TQ), 1) // CHUNK
    diag_mask = key_chunk <= qry_chunk

    def scores_t(hh, j):
        kb = k_ref[0, hh, pl.ds(pl.multiple_of(j * TK, TK), TK), :]
        return lax.dot_general(kb, q_ref[0, hh], (((1,), (1,)), ((), ())),
                               preferred_element_type=F32)

    def values_t(hh, j):
        return vt_ref[0, j, hh * V_DIM:(hh + 1) * V_DIM, :]

    for hh in range(2):
        s = jnp.where(diag_mask, scores_t(hh, i), -jnp.inf)
        m = jnp.max(s, axis=0, keepdims=True)
        p = jnp.exp2(s - m)
        m_ref[hh] = m
        l_ref[hh] = jnp.sum(p, axis=0, keepdims=True)
        acc_ref[hh] = _dot(values_t(hh, i), p.astype(BF16))

    def step(j, carry):
        for hh in range(2):
            s = scores_t(hh, j)
            m_old = m_ref[hh]
            m_new = jnp.maximum(m_old, jnp.max(s, axis=0, keepdims=True))
            alpha = jnp.exp2(m_old - m_new)
            p = jnp.exp2(s - m_new)
            l_ref[hh] = alpha * l_ref[hh] + jnp.sum(p, axis=0, keepdims=True)
            acc_ref[hh] = alpha * acc_ref[hh] + _dot(values_t(hh, j), p.astype(BF16))
            m_ref[hh] = m_new
        return carry

    lax.fori_loop(0, i, step, 0)

    o_t = jnp.concatenate([acc_ref[0] / l_ref[0], acc_ref[1] / l_ref[1]], axis=0)
    o_ref[0] = o_t.T.astype(BF16)


def _attn_bounded_body(q_ref, k_ref, vt_ref, o_ref):
    qb = ATTN_Q_BLOCK
    n_tiles = q_ref.shape[2] // qb

    n_sub = qb // TK

    def diag_mask(nq):
        key_chunk = lax.broadcasted_iota(jnp.int32, (TK, nq), 0) // CHUNK
        qry_chunk = lax.broadcasted_iota(jnp.int32, (TK, nq), 1) // CHUNK
        return key_chunk <= qry_chunk

    masks = {qb - a * TK: diag_mask(qb - a * TK) for a in range(n_sub)}

    for i in range(n_tiles):
        q_lo = i * qb
        sums = [[None] * n_sub for _ in range(2)]
        accs = [[None] * n_sub for _ in range(2)]
        for hh in range(2):
            for j in range((q_lo + qb) // TK):
                first = max(0, j - q_lo // TK)
                nq = qb - first * TK
                q = q_ref[0, hh, q_lo + first * TK:q_lo + qb, :]
                kb = k_ref[0, hh, j * TK:(j + 1) * TK, :]
                p = jnp.exp2(lax.dot_general(kb, q, (((1,), (1,)), ((), ())),
                                             preferred_element_type=F32))
                if j * TK >= q_lo:
                    p = jnp.where(masks[nq], p, 0.0)
                ps = jnp.sum(p.reshape(TK // SUBLANES, SUBLANES, nq), axis=0)
                pv = _dot(vt_ref[0, j, hh * V_DIM:(hh + 1) * V_DIM, :], p.astype(BF16))
                for a in range(first, n_sub):
                    cols = slice((a - first) * TK, (a - first + 1) * TK)
                    sums[hh][a] = ps[:, cols] if sums[hh][a] is None else sums[hh][a] + ps[:, cols]
                    accs[hh][a] = pv[:, cols] if accs[hh][a] is None else accs[hh][a] + pv[:, cols]
        for a in range(n_sub):
            o_t = jnp.concatenate(
                [accs[hh][a] / jnp.sum(sums[hh][a], axis=0, keepdims=True) for hh in range(2)],
                axis=0)
            o_ref[0, q_lo + a * TK:q_lo + (a + 1) * TK, :] = o_t.T.astype(BF16)


def _attention_bounded(q, k, vt):
    b, _, s, _ = q.shape
    return pl.pallas_call(
        _attn_bounded_body,
        name="attn_bounded",
        grid=(b, N_HEADS // 2),
        in_specs=[pl.BlockSpec((1, 2, s, HEAD_PAD), lambda bi, p: (bi, p, 0, 0)),
                  pl.BlockSpec((1, 2, s, HEAD_PAD), lambda bi, p: (bi, p, 0, 0)),
                  pl.BlockSpec((1, s // TK, 2 * V_DIM, TK), lambda bi, p: (bi, 0, p, 0))],
        out_specs=pl.BlockSpec((1, s, 2 * V_DIM), lambda bi, p: (bi, 0, p)),
        out_shape=jax.ShapeDtypeStruct((b, s, N_HEADS * V_DIM), BF16),
        compiler_params=pltpu.CompilerParams(
            dimension_semantics=("arbitrary", "arbitrary"),
            vmem_limit_bytes=VMEM_LIMIT),
    )(q, k, vt)


def _attention(q, k, vt):
    b, _, s, _ = q.shape
    assert TQ == TK
    scratch = [pltpu.VMEM((2, 1, TQ), F32), pltpu.VMEM((2, 1, TQ), F32),
               pltpu.VMEM((2, V_DIM, TQ), F32)]
    return pl.pallas_call(
        _attn_body,
        name="attn",
        grid=(b, N_HEADS // 2, s // TQ),
        in_specs=[pl.BlockSpec((1, 2, TQ, HEAD_PAD), lambda bi, p, i: (bi, p, i, 0)),
                  pl.BlockSpec((1, 2, s, HEAD_PAD), lambda bi, p, i: (bi, p, 0, 0)),
                  pl.BlockSpec((1, s // TK, 2 * V_DIM, TK), lambda bi, p, i: (bi, 0, p, 0))],
        out_specs=pl.BlockSpec((1, TQ, 2 * V_DIM), lambda bi, p, i: (bi, i, p)),
        out_shape=jax.ShapeDtypeStruct((b, s, N_HEADS * V_DIM), BF16),
        scratch_shapes=scratch,
        compiler_params=pltpu.CompilerParams(
            dimension_semantics=("arbitrary", "arbitrary", "arbitrary"),
            vmem_limit_bytes=VMEM_LIMIT),
    )(q, k, vt)


def _conv_fill(u, halo, ubuf_ref, shift_ref):
    ubuf_ref[0:HALO, :] = halo
    ubuf_ref[HALO:HALO + TM_MIX, :] = u
    for bb in range(1, SUBLANES):
        shift_ref[bb - 1] = ubuf_ref[bb:bb + SHIFT_ROWS, :]


def _conv_branch(row0, nrows, cw_ref, cb_ref, lng_ref, lnb_ref, ubuf_ref, shift_ref, conv_ref):
    first = HALO - (CONV_WIDTH - 1)
    for cb in range(C_CONV // LANES):
        cols = slice(cb * LANES, (cb + 1) * LANES)
        for rb in range(row0 // CONV_ROWS, (row0 + nrows) // CONV_ROWS):
            acc = None
            for kk in range(CONV_WIDTH):
                aa, bb = divmod(first + kk, SUBLANES)
                r0 = aa * SUBLANES + rb * CONV_ROWS
                if bb == 0:
                    win = ubuf_ref[r0:r0 + CONV_ROWS, cols]
                else:
                    win = shift_ref[bb - 1, r0:r0 + CONV_ROWS, cols]
                term = cw_ref[kk:kk + 1, cols] * win
                acc = term if acc is None else acc + term
            conv_ref[rb * CONV_ROWS:(rb + 1) * CONV_ROWS, cols] = acc + cb_ref[:, cols]
    conv = conv_ref[row0:row0 + nrows, :]
    mu = jnp.mean(conv, axis=-1, keepdims=True)
    cc = conv - mu
    ln = cc * lax.rsqrt(jnp.mean(cc * cc, axis=-1, keepdims=True) + EPS)
    ln = ln * lng_ref[...] + lnb_ref[...]
    return (ln * jax.nn.sigmoid(ln)).astype(BF16)


def _mix_body(x_ref, h_ref, u_ref, halo_ref, o_ref, wg_ref, gbias_ref,
              cw_ref, cb_ref, lng_ref, lnb_ref, wco_ref, wmo_ref, wout_ref,
              out_ref, ubuf_ref, shift_ref, conv_ref):
    i = pl.program_id(1)
    halo = halo_ref[0]
    halo = jnp.where(i > 0, halo, jnp.zeros_like(halo))
    _conv_fill(u_ref[0], halo, ubuf_ref, shift_ref)
    act = _conv_branch(0, TM_MIX, cw_ref, cb_ref, lng_ref, lnb_ref,
                       ubuf_ref, shift_ref, conv_ref)
    gates = jax.nn.sigmoid(_dot(h_ref[0], wg_ref[...]) + gbias_ref[...])
    y_conv = _dot(act, wco_ref[...])
    y_mla = _dot(o_ref[0], wmo_ref[...])
    y = gates[:, :D_MODEL] * y_conv + gates[:, D_MODEL:] * y_mla
    out_ref[0] = x_ref[0] + _dot(y.astype(BF16), wout_ref[...])


def _mix(x3, h, u, o, w_gates, gbias, cw, cb, lng, lnb, wco, wmo, wout):
    b, s, _ = x3.shape
    tm = TM_MIX
    tok = lambda w: pl.BlockSpec((1, tm, w), lambda bi, i: (bi, i, 0))
    halo = pl.BlockSpec(
        (1, HALO, C_CONV), lambda bi, i: (bi, jnp.maximum(i * (tm // HALO) - 1, 0), 0))
    consts = [w_gates, gbias, cw, cb, lng, lnb, wco, wmo, wout]
    return pl.pallas_call(
        _mix_body,
        name="mix",
        grid=(b, s // tm),
        in_specs=[tok(D_MODEL), tok(D_MODEL), tok(C_CONV), halo, tok(N_HEADS * V_DIM)]
                 + [_const_spec(c.shape) for c in consts],
        out_specs=tok(D_MODEL),
        out_shape=jax.ShapeDtypeStruct((b, s, D_MODEL), F32),
        scratch_shapes=[pltpu.VMEM((HALO + tm, C_CONV), F32),
                        pltpu.VMEM((SUBLANES - 1, SHIFT_ROWS, C_CONV), F32),
                        pltpu.VMEM((tm, C_CONV), F32)],
        compiler_params=pltpu.CompilerParams(
            dimension_semantics=("arbitrary", "arbitrary"),
            vmem_limit_bytes=VMEM_LIMIT),
    )(x3, h, u, u, o, *consts)


def kernel(x, positions, ffn1_norm, ffn1_w_gate, ffn1_w_up, ffn1_w_down, mix_norm, w_in,
           gate_bias, conv_w, conv_b, conv_ln_g, conv_ln_b, w_conv_out, cq_norm, ckv_norm,
           w_uq, w_ukv, q_norm, k_norm, w_mla_out, w_out, ffn2_norm, ffn2_w_gate, ffn2_w_up,
           ffn2_w_down):
    b, s, d = x.shape
    depth = ffn1_norm.shape[0]
    assert d == D_MODEL and s % max(TM_PROJ, TM_MIX, TQ, ATTN_Q_BLOCK) == 0
    assert (b * s) % TM_FFN == 0 and ATTN_Q_BLOCK % TK == 0
    row = lambda v: v.reshape(1, -1)
    cos_a, sin_b = _rope_tables(positions)
    ffn1_w = [w.astype(BF16) for w in (ffn1_w_gate, ffn1_w_up, ffn1_w_down)]
    ffn2_w = [w.astype(BF16) for w in (ffn2_w_gate, ffn2_w_up, ffn2_w_down)]
    for l in range(depth):
        x = _ffn(x.reshape(b * s, d), row(ffn1_norm[l]), *ffn1_w, l).reshape(b, s, d)

        wa, w_gates, wq, wk, wvt, ga, gb, gka, gkb, q_off, k_off, bounded = _prep_mixer_weights(
            w_in[l], w_uq[l], w_ukv[l], q_norm[l], k_norm[l])
        h, u, q, k, vt = _inproj(x, row(mix_norm[l]), wa, row(cq_norm[l]), row(ckv_norm[l]),
                                 wq, wk, wvt, ga, gb, gka, gkb, q_off, k_off, cos_a, sin_b)
        o = lax.cond(bounded, _attention_bounded, _attention, q, k, vt)
        x = _mix(x, h, u, o, w_gates, gate_bias[l].reshape(1, 2 * D_MODEL),
                 conv_w[l], row(conv_b[l]), row(conv_ln_g[l]), row(conv_ln_b[l]),
                 w_conv_out[l].astype(BF16), w_mla_out[l].astype(BF16),
                 w_out[l].astype(BF16))

        x = _ffn(x.reshape(b * s, d), row(ffn2_norm[l]), *ffn2_w, l).reshape(b, s, d)
    return x
```

```python
import functools
import math

import jax
import jax.numpy as jnp
from jax import lax
from jax.experimental import pallas as pl
from jax.experimental.pallas import tpu as pltpu

D_MODEL = 1024
D_FF = 2816
C_CONV = 512
CONV_WIDTH = 31
N_HEADS = 8
NOPE_DIM = 64
ROPE_DIM = 32
QK_DIM = NOPE_DIM + ROPE_DIM
V_DIM = 64
Q_LORA = 256
KV_LORA = 256
CHUNK = 64
ROPE_THETA = 10000.0
EPS = 1e-6

LANES = 128
SUBLANES = 8
MAX_FIXED_OFFSET = 50.0
HEAD_PAD = LANES
HALF_ROPE = ROPE_DIM // 2
VMEM_LIMIT = 56 * 1024 * 1024
FF_CHUNK = 256
N_FF_CHUNKS = D_FF // FF_CHUNK
TM_FFN = 1024
TM_PROJ = 512
TM_MIX = 512
TQ = 512
TK = TM_PROJ
ATTN_Q_BLOCK = 2048
HALO = 32
SHIFT_ROWS = TM_MIX + HALO - SUBLANES
CONV_ROWS = 64
F32 = jnp.float32
BF16 = jnp.bfloat16


def _const_spec(shape):
    zeros = (0,) * len(shape)
    return pl.BlockSpec(shape, lambda *_: zeros, pipeline_mode=pl.Buffered(1))


def _rms(x, g):
    ms = jnp.mean(x * x, axis=-1, keepdims=True)
    return x * lax.rsqrt(ms + EPS) * g


def _dot(a, b):
    return jnp.dot(a, b, preferred_element_type=F32)


def _rope_body(pos_ref, inv_ref, cos_ref, sin_ref):
    ang = pos_ref[...].astype(F32) * inv_ref[...]
    cos_ref[...] = jnp.cos(ang)
    sin_ref[...] = jnp.sin(ang)


def _rope_tables(positions):
    b, s = positions.shape
    t = b * s
    rows = t * HALF_ROPE // LANES
    inv_freq = ROPE_THETA ** (-jnp.arange(0, ROPE_DIM, 2, dtype=F32) / ROPE_DIM)
    pos_rep = jnp.broadcast_to(positions.reshape(t, 1), (t, HALF_ROPE)).reshape(rows, LANES)
    inv_rep = jnp.tile(inv_freq, LANES // HALF_ROPE).reshape(1, LANES)
    tr = 512
    cos, sin = pl.pallas_call(
        _rope_body,
        name="rope_tables",
        grid=(rows // tr,),
        in_specs=[pl.BlockSpec((tr, LANES), lambda i: (i, 0)),
                  pl.BlockSpec((1, LANES), lambda i: (0, 0))],
        out_specs=[pl.BlockSpec((tr, LANES), lambda i: (i, 0))] * 2,
        out_shape=[jax.ShapeDtypeStruct((rows, LANES), F32)] * 2,
    )(pos_rep, inv_rep)
    return cos.reshape(b, s, HALF_ROPE), sin.reshape(b, s, HALF_ROPE)


def _ffn_value(x, g, wg_ref, wu_ref, wd_ref):
    xn = _rms(x, g).astype(BF16)
    acc = None
    for c in range(N_FF_CHUNKS):
        cols = slice(c * FF_CHUNK, (c + 1) * FF_CHUNK)
        gate = _dot(xn, wg_ref[0, :, cols])
        up = _dot(xn, wu_ref[0, :, cols])
        a = (gate * jax.nn.sigmoid(gate) * up).astype(BF16)
        d = _dot(a, wd_ref[0, cols, :])
        acc = d if acc is None else acc + d
    return x + 0.5 * acc


def _ffn_body(x_ref, g_ref, wg_ref, wu_ref, wd_ref, o_ref):
    o_ref[...] = _ffn_value(x_ref[...], g_ref[...], wg_ref, wu_ref, wd_ref)


def _ffn(x2, g, w_gate, w_up, w_down, layer):
    t = x2.shape[0]
    tok = pl.BlockSpec((TM_FFN, D_MODEL), lambda i: (i, 0))
    layer_spec = lambda w: pl.BlockSpec((1,) + w.shape[1:], lambda i: (layer, 0, 0),
                                        pipeline_mode=pl.Buffered(1))
    return pl.pallas_call(
        _ffn_body,
        name="ffn",
        grid=(t // TM_FFN,),
        in_specs=[tok, _const_spec((1, D_MODEL)),
                  layer_spec(w_gate), layer_spec(w_up), layer_spec(w_down)],
        out_specs=tok,
        out_shape=jax.ShapeDtypeStruct((t, D_MODEL), F32),
        compiler_params=pltpu.CompilerParams(
            dimension_semantics=("arbitrary",), vmem_limit_bytes=VMEM_LIMIT),
    )(x2, g, w_gate, w_up, w_down)


def _inproj_body(x_ref, mixg_ref, wa_ref, cqg_ref, ckvg_ref, wq_ref, wk_ref, wvt_ref,
                 ga_ref, gb_ref, gka_ref, gkb_ref, qoff_ref, koff_ref, ca_ref, sb_ref,
                 h_ref, u_ref, q_ref, k_ref, vt_ref):
    h = _rms(x_ref[0], mixg_ref[...]).astype(BF16)
    h_ref[0] = h
    proj = _dot(h, wa_ref[...])
    o1 = 2 * C_CONV
    o2 = o1 + Q_LORA
    o3 = o2 + KV_LORA
    u_ref[0] = proj[:, :C_CONV] * jax.nn.sigmoid(proj[:, C_CONV:o1])
    cqn = _rms(proj[:, o1:o2], cqg_ref[...]).astype(BF16)
    ckvn = _rms(proj[:, o2:o3], ckvg_ref[...]).astype(BF16)
    kra = proj[:, o3:o3 + HEAD_PAD]
    krb = proj[:, o3 + HEAD_PAD:o3 + 2 * HEAD_PAD]
    qq = _dot(cqn, wq_ref[...])
    kv = _dot(ckvn, wk_ref[...])
    vt_ref[0, 0] = lax.dot_general(wvt_ref[...], ckvn, (((1,), (1,)), ((), ())),
                                   preferred_element_type=F32).astype(BF16)
    cos = ca_ref[0]
    sin = sb_ref[0]
    tm = cos.shape[0]
    ca = jnp.concatenate([jnp.ones((tm, NOPE_DIM), F32), cos, cos,
                          jnp.zeros((tm, HEAD_PAD - QK_DIM), F32)], axis=1)
    sb = jnp.concatenate([jnp.zeros((tm, NOPE_DIM), F32), sin, sin,
                          jnp.zeros((tm, HEAD_PAD - QK_DIM), F32)], axis=1)
    q_cos = ga_ref[...] * ca
    q_sin = gb_ref[...] * sb
    gka = gka_ref[...]
    k_rot = kra * (gka * ca) + krb * (gkb_ref[...] * sb)
    ss_rope = jnp.sum(kra * kra, axis=-1, keepdims=True)
    hw = N_HEADS * HEAD_PAD
    for hh in range(N_HEADS):
        lo = hh * HEAD_PAD
        qa = qq[:, lo:lo + HEAD_PAD]
        qb = qq[:, hw + lo:hw + lo + HEAD_PAD]
        rq = lax.rsqrt(jnp.sum(qa * qa, axis=-1, keepdims=True) / QK_DIM + EPS)
        q_ref[0, hh] = (rq * (qa * q_cos + qb * q_sin) + qoff_ref[...]).astype(BF16)
        kn = kv[:, lo:lo + HEAD_PAD]
        ss = jnp.sum(kn * kn, axis=-1, keepdims=True) + ss_rope
        rk = lax.rsqrt(ss / QK_DIM + EPS)
        k_ref[0, hh] = (rk * (kn * gka + k_rot) + koff_ref[...]).astype(BF16)


def _inproj(x3, mixg, wa, cqg, ckvg, wq, wk, wvt, ga, gb, gka, gkb, q_off, k_off,
            cos, sin):
    b, s, _ = x3.shape
    tm = TM_PROJ
    hv = N_HEADS * V_DIM
    tok = lambda w: pl.BlockSpec((1, tm, w), lambda bi, i: (bi, i, 0))
    head = pl.BlockSpec((1, N_HEADS, tm, HEAD_PAD), lambda bi, i: (bi, 0, i, 0))
    vt_spec = pl.BlockSpec((1, 1, hv, tm), lambda bi, i: (bi, i, 0, 0))
    return pl.pallas_call(
        _inproj_body,
        name="inproj",
        grid=(b, s // tm),
        in_specs=[tok(D_MODEL), _const_spec(mixg.shape), _const_spec(wa.shape),
                  _const_spec(cqg.shape), _const_spec(ckvg.shape),
                  _const_spec(wq.shape), _const_spec(wk.shape), _const_spec(wvt.shape),
                  _const_spec(ga.shape), _const_spec(gb.shape),
                  _const_spec(gka.shape), _const_spec(gkb.shape),
                  _const_spec(q_off.shape), _const_spec(k_off.shape),
                  tok(HALF_ROPE), tok(HALF_ROPE)],
        out_specs=[tok(D_MODEL), tok(C_CONV), head, head, vt_spec],
        out_shape=[jax.ShapeDtypeStruct((b, s, D_MODEL), BF16),
                   jax.ShapeDtypeStruct((b, s, C_CONV), F32),
                   jax.ShapeDtypeStruct((b, N_HEADS, s, HEAD_PAD), BF16),
                   jax.ShapeDtypeStruct((b, N_HEADS, s, HEAD_PAD), BF16),
                   jax.ShapeDtypeStruct((b, s // tm, hv, tm), BF16)],
        compiler_params=pltpu.CompilerParams(
            dimension_semantics=("arbitrary", "arbitrary"),
            vmem_limit_bytes=VMEM_LIMIT),
    )(x3, mixg, wa, cqg, ckvg, wq, wk, wvt, ga, gb, gka, gkb, q_off, k_off, cos, sin)


def _pad_lanes(w, lo, total=HEAD_PAD):
    pad = [(0, 0)] * (w.ndim - 1) + [(lo, total - lo - w.shape[-1])]
    return jnp.pad(w, pad)


def _swap_halves(w):
    return jnp.concatenate([w[..., HALF_ROPE:], w[..., :HALF_ROPE]], axis=-1)


def _prep_mixer_weights(w_in, w_uq, w_ukv, q_norm, k_norm):
    o1 = 2 * C_CONV
    o3 = o1 + Q_LORA + KV_LORA
    o4 = o3 + ROPE_DIM
    w_rope = w_in[:, o3:o4]
    wa = jnp.concatenate(
        [w_in[:, :o3], _pad_lanes(w_rope, NOPE_DIM),
         _pad_lanes(_swap_halves(w_rope), NOPE_DIM)], axis=-1).astype(BF16)
    w_gates = w_in[:, o4:].astype(BF16)

    wq3 = w_uq.reshape(Q_LORA, N_HEADS, QK_DIM)
    wq_a = _pad_lanes(wq3, 0).reshape(Q_LORA, N_HEADS * HEAD_PAD)
    wq_b = _pad_lanes(_swap_halves(wq3[..., NOPE_DIM:]), NOPE_DIM)
    wq_b = wq_b.reshape(Q_LORA, N_HEADS * HEAD_PAD)
    wq = jnp.concatenate([wq_a, wq_b], axis=-1).astype(BF16)

    wkv3 = w_ukv.reshape(KV_LORA, N_HEADS, NOPE_DIM + V_DIM)
    wk = _pad_lanes(wkv3[..., :NOPE_DIM], 0).reshape(KV_LORA, N_HEADS * HEAD_PAD)
    wk = wk.astype(BF16)
    wvt = wkv3[..., NOPE_DIM:].reshape(KV_LORA, N_HEADS * V_DIM).T.astype(BF16)

    q_scale = QK_DIM ** -0.5 * math.log2(math.e)
    sign = jnp.concatenate([-jnp.ones((HALF_ROPE,), F32), jnp.ones((HALF_ROPE,), F32)])
    ga = _pad_lanes(q_norm * q_scale, 0).reshape(1, HEAD_PAD)
    gb = _pad_lanes(_swap_halves(q_norm[NOPE_DIM:]) * sign * q_scale, NOPE_DIM)
    gb = gb.reshape(1, HEAD_PAD)
    gka = _pad_lanes(k_norm, 0).reshape(1, HEAD_PAD)
    gkb = _pad_lanes(_swap_halves(k_norm[NOPE_DIM:]) * sign, NOPE_DIM).reshape(1, HEAD_PAD)

    bound = QK_DIM * q_scale * jnp.max(jnp.abs(q_norm)) * jnp.max(jnp.abs(k_norm))
    bounded = bound <= MAX_FIXED_OFFSET
    lane = jnp.arange(HEAD_PAD) == QK_DIM
    q_off = jnp.where(lane & bounded, -bound, 0.0).astype(F32).reshape(1, HEAD_PAD)
    k_off = jnp.where(lane, 1.0, 0.0).astype(F32).reshape(1, HEAD_PAD)
    return wa, w_gates, wq, wk, wvt, ga, gb, gka, gkb, q_off, k_off, bounded


def _attn_body(q_ref, k_ref, vt_ref, o_ref, m_ref, l_ref, acc_ref):
    i = pl.program_id(2)
    key_chunk = lax.broadcasted_iota(jnp.int32, (TK, TQ), 0) // CHUNK
    qry_chunk = lax.broadcasted_iota(jnp.int32, (TK, TQ), 1) // CHUNK
    diag_mask = key_chunk <= qry_chunk

    def scores_t(hh, j):
        kb = k_ref[0, hh, pl.ds(pl.multiple_of(j * TK, TK), TK), :]
        return lax.dot_general(kb, q_ref[0, hh], (((1,), (1,)), ((), ())),
                               preferred_element_type=F32)

    def values_t(hh, j):
        return vt_ref[0, j, hh * V_DIM:(hh + 1) * V_DIM, :]

    for hh in range(2):
        s = jnp.where(diag_mask, scores_t(hh, i), -jnp.inf)
        m = jnp.max(s, axis=0, keepdims=True)
        p = jnp.exp2(s - m)
        m_ref[hh] = m
        l_ref[hh] = jnp.sum(p, axis=0, keepdims=True)
        acc_ref[hh] = _dot(values_t(hh, i), p.astype(BF16))

    def step(j, carry):
        for hh in range(2):
            s = scores_t(hh, j)
            m_old = m_ref[hh]
            m_new = jnp.maximum(m_old, jnp.max(s, axis=0, keepdims=True))
            alpha = jnp.exp2(m_old - m_new)
            p = jnp.exp2(s - m_new)
            l_ref[hh] = alpha * l_ref[hh] + jnp.sum(p, axis=0, keepdims=True)
            acc_ref[hh] = alpha * acc_ref[hh] + _dot(values_t(hh, j), p.astype(BF16))
            m_ref[hh] = m_new
        return carry

    lax.fori_loop(0, i, step, 0)

    o_t = jnp.concatenate([acc_ref[0] / l_ref[0], acc_ref[1] / l_ref[1]], axis=0)
    o_ref[0] = o_t.T.astype(BF16)


def _attn_bounded_body(q_ref, k_ref, vt_ref, o_ref):
    qb = ATTN_Q_BLOCK
    n_tiles = q_ref.shape[2] // qb

    n_sub = qb // TK

    def diag_mask(nq):
        key_chunk = lax.broadcasted_iota(jnp.int32, (TK, nq), 0) // CHUNK
        qry_chunk = lax.broadcasted_iota(jnp.int32, (TK, nq), 1) // CHUNK
        return key_chunk <= qry_chunk

    masks = {qb - a * TK: diag_mask(qb - a * TK) for a in range(n_sub)}

    for i in range(n_tiles):
        q_lo = i * qb
        sums = [[None] * n_sub for _ in range(2)]
        accs = [[None] * n_sub for _ in range(2)]
        for hh in range(2):
            for j in range((q_lo + qb) // TK):
                first = max(0, j - q_lo // TK)
                nq = qb - first * TK
                q = q_ref[0, hh, q_lo + first * TK:q_lo + qb, :]
                kb = k_ref[0, hh, j * TK:(j + 1) * TK, :]
                p = jnp.exp2(lax.dot_general(kb, q, (((1,), (1,)), ((), ())),
                                             preferred_element_type=F32))
                if j * TK >= q_lo:
                    p = jnp.where(masks[nq], p, 0.0)
                ps = jnp.sum(p.reshape(TK // SUBLANES, SUBLANES, nq), axis=0)
                pv = _dot(vt_ref[0, j, hh * V_DIM:(hh + 1) * V_DIM, :], p.astype(BF16))
                for a in range(first, n_sub):
                    cols = slice((a - first) * TK, (a - first + 1) * TK)
                    sums[hh][a] = ps[:, cols] if sums[hh][a] is None else sums[hh][a] + ps[:, cols]
                    accs[hh][a] = pv[:, cols] if accs[hh][a] is None else accs[hh][a] + pv[:, cols]
        for a in range(n_sub):
            o_t = jnp.concatenate(
                [accs[hh][a] / jnp.sum(sums[hh][a], axis=0, keepdims=True) for hh in range(2)],
                axis=0)
            o_ref[0, q_lo + a * TK:q_lo + (a + 1) * TK, :] = o_t.T.astype(BF16)


def _attention_bounded(q, k, vt):
    b, _, s, _ = q.shape
    return pl.pallas_call(
        _attn_bounded_body,
        name="attn_bounded",
        grid=(b, N_HEADS // 2),
        in_specs=[pl.BlockSpec((1, 2, s, HEAD_PAD), lambda bi, p: (bi, p, 0, 0)),
                  pl.BlockSpec((1, 2, s, HEAD_PAD), lambda bi, p: (bi, p, 0, 0)),
                  pl.BlockSpec((1, s // TK, 2 * V_DIM, TK), lambda bi, p: (bi, 0, p, 0))],
        out_specs=pl.BlockSpec((1, s, 2 * V_DIM), lambda bi, p: (bi, 0, p)),
        out_shape=jax.ShapeDtypeStruct((b, s, N_HEADS * V_DIM), BF16),
        compiler_params=pltpu.CompilerParams(
            dimension_semantics=("arbitrary", "arbitrary"),
            vmem_limit_bytes=VMEM_LIMIT),
    )(q, k, vt)


def _attention(q, k, vt):
    b, _, s, _ = q.shape
    assert TQ == TK
    scratch = [pltpu.VMEM((2, 1, TQ), F32), pltpu.VMEM((2, 1, TQ), F32),
               pltpu.VMEM((2, V_DIM, TQ), F32)]
    return pl.pallas_call(
        _attn_body,
        name="attn",
        grid=(b, N_HEADS // 2, s // TQ),
        in_specs=[pl.BlockSpec((1, 2, TQ, HEAD_PAD), lambda bi, p, i: (bi, p, i, 0)),
                  pl.BlockSpec((1, 2, s, HEAD_PAD), lambda bi, p, i: (bi, p, 0, 0)),
                  pl.BlockSpec((1, s // TK, 2 * V_DIM, TK), lambda bi, p, i: (bi, 0, p, 0))],
        out_specs=pl.BlockSpec((1, TQ, 2 * V_DIM), lambda bi, p, i: (bi, i, p)),
        out_shape=jax.ShapeDtypeStruct((b, s, N_HEADS * V_DIM), BF16),
        scratch_shapes=scratch,
        compiler_params=pltpu.CompilerParams(
            dimension_semantics=("arbitrary", "arbitrary", "arbitrary"),
            vmem_limit_bytes=VMEM_LIMIT),
    )(q, k, vt)


def _conv_fill(u, halo, ubuf_ref, shift_ref):
    ubuf_ref[0:HALO, :] = halo
    ubuf_ref[HALO:HALO + TM_MIX, :] = u
    for bb in range(1, SUBLANES):
        shift_ref[bb - 1] = ubuf_ref[bb:bb + SHIFT_ROWS, :]


def _conv_branch(row0, nrows, cw_ref, cb_ref, lng_ref, lnb_ref, ubuf_ref, shift_ref, conv_ref):
    first = HALO - (CONV_WIDTH - 1)
    for cb in range(C_CONV // LANES):
        cols = slice(cb * LANES, (cb + 1) * LANES)
        for rb in range(row0 // CONV_ROWS, (row0 + nrows) // CONV_ROWS):
            acc = None
            for kk in range(CONV_WIDTH):
                aa, bb = divmod(first + kk, SUBLANES)
                r0 = aa * SUBLANES + rb * CONV_ROWS
                if bb == 0:
                    win = ubuf_ref[r0:r0 + CONV_ROWS, cols]
                else:
                    win = shift_ref[bb - 1, r0:r0 + CONV_ROWS, cols]
                term = cw_ref[kk:kk + 1, cols] * win
                acc = term if acc is None else acc + term
            conv_ref[rb * CONV_ROWS:(rb + 1) * CONV_ROWS, cols] = acc + cb_ref[:, cols]
    conv = conv_ref[row0:row0 + nrows, :]
    mu = jnp.mean(conv, axis=-1, keepdims=True)
    cc = conv - mu
    ln = cc * lax.rsqrt(jnp.mean(cc * cc, axis=-1, keepdims=True) + EPS)
    ln = ln * lng_ref[...] + lnb_ref[...]
    return (ln * jax.nn.sigmoid(ln)).astype(BF16)


def _mix_body(x_ref, h_ref, u_ref, halo_ref, o_ref, wg_ref, gbias_ref,
              cw_ref, cb_ref, lng_ref, lnb_ref, wco_ref, wmo_ref, wout_ref,
              out_ref, ubuf_ref, shift_ref, conv_ref):
    i = pl.program_id(1)
    halo = halo_ref[0]
    halo = jnp.where(i > 0, halo, jnp.zeros_like(halo))
    _conv_fill(u_ref[0], halo, ubuf_ref, shift_ref)
    act = _conv_branch(0, TM_MIX, cw_ref, cb_ref, lng_ref, lnb_ref,
                       ubuf_ref, shift_ref, conv_ref)
    gates = jax.nn.sigmoid(_dot(h_ref[0], wg_ref[...]) + gbias_ref[...])
    y_conv = _dot(act, wco_ref[...])
    y_mla = _dot(o_ref[0], wmo_ref[...])
    y = gates[:, :D_MODEL] * y_conv + gates[:, D_MODEL:] * y_mla
    out_ref[0] = x_ref[0] + _dot(y.astype(BF16), wout_ref[...])


def _mix(x3, h, u, o, w_gates, gbias, cw, cb, lng, lnb, wco, wmo, wout):
    b, s, _ = x3.shape
    tm = TM_MIX
    tok = lambda w: pl.BlockSpec((1, tm, w), lambda bi, i: (bi, i, 0))
    halo = pl.BlockSpec(
        (1, HALO, C_CONV), lambda bi, i: (bi, jnp.maximum(i * (tm // HALO) - 1, 0), 0))
    consts = [w_gates, gbias, cw, cb, lng, lnb, wco, wmo, wout]
    return pl.pallas_call(
        _mix_body,
        name="mix",
        grid=(b, s // tm),
        in_specs=[tok(D_MODEL), tok(D_MODEL), tok(C_CONV), halo, tok(N_HEADS * V_DIM)]
                 + [_const_spec(c.shape) for c in consts],
        out_specs=tok(D_MODEL),
        out_shape=jax.ShapeDtypeStruct((b, s, D_MODEL), F32),
        scratch_shapes=[pltpu.VMEM((HALO + tm, C_CONV), F32),
                        pltpu.VMEM((SUBLANES - 1, SHIFT_ROWS, C_CONV), F32),
                        pltpu.VMEM((tm, C_CONV), F32)],
        compiler_params=pltpu.CompilerParams(
            dimension_semantics=("arbitrary", "arbitrary"),
            vmem_limit_bytes=VMEM_LIMIT),
    )(x3, h, u, u, o, *consts)


def kernel(x, positions, ffn1_norm, ffn1_w_gate, ffn1_w_up, ffn1_w_down, mix_norm, w_in,
           gate_bias, conv_w, conv_b, conv_ln_g, conv_ln_b, w_conv_out, cq_norm, ckv_norm,
           w_uq, w_ukv, q_norm, k_norm, w_mla_out, w_out, ffn2_norm, ffn2_w_gate, ffn2_w_up,
           ffn2_w_down):
    b, s, d = x.shape
    depth = ffn1_norm.shape[0]
    assert d == D_MODEL and s % max(TM_PROJ, TM_MIX, TQ, ATTN_Q_BLOCK) == 0
    assert (b * s) % TM_FFN == 0 and ATTN_Q_BLOCK % TK == 0
    row = lambda v: v.reshape(1, -1)
    cos, sin = _rope_tables(positions)
    ffn1_w = [w.astype(BF16) for w in (ffn1_w_gate, ffn1_w_up, ffn1_w_down)]
    ffn2_w = [w.astype(BF16) for w in (ffn2_w_gate, ffn2_w_up, ffn2_w_down)]
    for l in range(depth):
        x = _ffn(x.reshape(b * s, d), row(ffn1_norm[l]), *ffn1_w, l).reshape(b, s, d)

        wa, w_gates, wq, wk, wvt, ga, gb, gka, gkb, q_off, k_off, bounded = _prep_mixer_weights(
            w_in[l], w_uq[l], w_ukv[l], q_norm[l], k_norm[l])
        h, u, q, k, vt = _inproj(x, row(mix_norm[l]), wa, row(cq_norm[l]), row(ckv_norm[l]),
                                 wq, wk, wvt, ga, gb, gka, gkb, q_off, k_off, cos, sin)
        o = lax.cond(bounded, _attention_bounded, _attention, q, k, vt)
        x = _mix(x, h, u, o, w_gates, gate_bias[l].reshape(1, 2 * D_MODEL),
                 conv_w[l], row(conv_b[l]), row(conv_ln_g[l]), row(conv_ln_b[l]),
                 w_conv_out[l].astype(BF16), w_mla_out[l].astype(BF16),
                 w_out[l].astype(BF16))

        x = _ffn(x.reshape(b * s, d), row(ffn2_norm[l]), *ffn2_w, l).reshape(b, s, d)
    return x
```

```python
import math

import jax
import jax.numpy as jnp
from jax import lax
from jax.experimental import pallas as pl
from jax.experimental.pallas import tpu as pltpu

D_MODEL = 1024
D_FF = 2816
C_CONV = 512
CONV_WIDTH = 31
N_HEADS = 8
NOPE_DIM = 64
ROPE_DIM = 32
QK_DIM = NOPE_DIM + ROPE_DIM
V_DIM = 64
Q_LORA = 256
KV_LORA = 256
CHUNK = 64
ROPE_THETA = 10000.0
EPS = 1e-6

LANES = 128
SUBLANES = 8
MAX_FIXED_OFFSET = 50.0
HEAD_PAD = LANES
HALF_ROPE = ROPE_DIM // 2
VMEM_LIMIT = 56 * 1024 * 1024
FF_CHUNK = 256
N_FF_CHUNKS = D_FF // FF_CHUNK
TM_FFN = 1024
TM_PROJ = 512
TM_MIX = 512
TQ = 512
TK = TM_PROJ
ATTN_Q_BLOCK = 2048
HALO = 32
SHIFT_ROWS = TM_MIX + HALO - SUBLANES
CONV_ROWS = 64
F32 = jnp.float32
BF16 = jnp.bfloat16


def _const_spec(shape):
    zeros = (0,) * len(shape)
    return pl.BlockSpec(shape, lambda *_: zeros, pipeline_mode=pl.Buffered(1))


def _rms(x, g):
    ms = jnp.mean(x * x, axis=-1, keepdims=True)
    return x * lax.rsqrt(ms + EPS) * g


def _dot(a, b):
    return jnp.dot(a, b, preferred_element_type=F32)


def _rope_body(pos_ref, inv_ref, cos_ref, sin_ref):
    ang = pos_ref[...].astype(F32) * inv_ref[...]
    cos_ref[...] = jnp.cos(ang)
    sin_ref[...] = jnp.sin(ang)


def _rope_tables(positions):
    b, s = positions.shape
    t = b * s
    rows = t * HALF_ROPE // LANES
    inv_freq = ROPE_THETA ** (-jnp.arange(0, ROPE_DIM, 2, dtype=F32) / ROPE_DIM)
    pos_rep = jnp.broadcast_to(positions.reshape(t, 1), (t, HALF_ROPE)).reshape(rows, LANES)
    inv_rep = jnp.tile(inv_freq, LANES // HALF_ROPE).reshape(1, LANES)
    tr = 512
    cos, sin = pl.pallas_call(
        _rope_body,
        name="rope_tables",
        grid=(rows // tr,),
        in_specs=[pl.BlockSpec((tr, LANES), lambda i: (i, 0)),
                  pl.BlockSpec((1, LANES), lambda i: (0, 0))],
        out_specs=[pl.BlockSpec((tr, LANES), lambda i: (i, 0))] * 2,
        out_shape=[jax.ShapeDtypeStruct((rows, LANES), F32)] * 2,
    )(pos_rep, inv_rep)
    return cos.reshape(b, s, HALF_ROPE), sin.reshape(b, s, HALF_ROPE)


def _ffn_value(x, g, wg_ref, wu_ref, wd_ref):
    xn = _rms(x, g).astype(BF16)
    acc = None
    for c in range(N_FF_CHUNKS):
        cols = slice(c * FF_CHUNK, (c + 1) * FF_CHUNK)
        gate = _dot(xn, wg_ref[0, :, cols])
        up = _dot(xn, wu_ref[0, :, cols])
        a = (gate * jax.nn.sigmoid(gate) * up).astype(BF16)
        d = _dot(a, wd_ref[0, cols, :])
        acc = d if acc is None else acc + d
    return x + 0.5 * acc


def _ffn_body(x_ref, g_ref, wg_ref, wu_ref, wd_ref, o_ref):
    o_ref[...] = _ffn_value(x_ref[...], g_ref[...], wg_ref, wu_ref, wd_ref)


def _ffn(x2, g, w_gate, w_up, w_down, layer):
    t = x2.shape[0]
    tok = pl.BlockSpec((TM_FFN, D_MODEL), lambda i: (i, 0))
    layer_spec = lambda w: pl.BlockSpec((1,) + w.shape[1:], lambda i: (layer, 0, 0),
                                        pipeline_mode=pl.Buffered(1))
    return pl.pallas_call(
        _ffn_body,
        name="ffn",
        grid=(t // TM_FFN,),
        in_specs=[tok, _const_spec((1, D_MODEL)),
                  layer_spec(w_gate), layer_spec(w_up), layer_spec(w_down)],
        out_specs=tok,
        out_shape=jax.ShapeDtypeStruct((t, D_MODEL), F32),
        compiler_params=pltpu.CompilerParams(
            dimension_semantics=("arbitrary",), vmem_limit_bytes=VMEM_LIMIT),
    )(x2, g, w_gate, w_up, w_down)


def _inproj_body(x_ref, mixg_ref, wa_ref, cqg_ref, ckvg_ref, wq_ref, wk_ref, wvt_ref,
                 ga_ref, gb_ref, gka_ref, gkb_ref, qoff_ref, koff_ref, ca_ref, sb_ref,
                 h_ref, u_ref, q_ref, k_ref, vt_ref):
    h = _rms(x_ref[0], mixg_ref[...]).astype(BF16)
    h_ref[0] = h
    proj = _dot(h, wa_ref[...])
    o1 = 2 * C_CONV
    o2 = o1 + Q_LORA
    o3 = o2 + KV_LORA
    u_ref[0] = proj[:, :C_CONV] * jax.nn.sigmoid(proj[:, C_CONV:o1])
    cqn = _rms(proj[:, o1:o2], cqg_ref[...]).astype(BF16)
    ckvn = _rms(proj[:, o2:o3], ckvg_ref[...]).astype(BF16)
    kra = proj[:, o3:o3 + HEAD_PAD]
    krb = proj[:, o3 + HEAD_PAD:o3 + 2 * HEAD_PAD]
    qq = _dot(cqn, wq_ref[...])
    kv = _dot(ckvn, wk_ref[...])
    vt_ref[0, 0] = lax.dot_general(wvt_ref[...], ckvn, (((1,), (1,)), ((), ())),
                                   preferred_element_type=F32).astype(BF16)
    cos = ca_ref[0]
    sin = sb_ref[0]
    tm = cos.shape[0]
    ca = jnp.concatenate([jnp.ones((tm, NOPE_DIM), F32), cos, cos,
                          jnp.zeros((tm, HEAD_PAD - QK_DIM), F32)], axis=1)
    sb = jnp.concatenate([jnp.zeros((tm, NOPE_DIM), F32), sin, sin,
                          jnp.zeros((tm, HEAD_PAD - QK_DIM), F32)], axis=1)
    q_cos = ga_ref[...] * ca
    q_sin = gb_ref[...] * sb
    gka = gka_ref[...]
    k_rot = kra * (gka * ca) + krb * (gkb_ref[...] * sb)
    ss_rope = jnp.sum(kra * kra, axis=-1, keepdims=True)
    hw = N_HEADS * HEAD_PAD
    for hh in range(N_HEADS):
        lo = hh * HEAD_PAD
        qa = qq[:, lo:lo + HEAD_PAD]
        qb = qq[:, hw + lo:hw + lo + HEAD_PAD]
        rq = lax.rsqrt(jnp.sum(qa * qa, axis=-1, keepdims=True) / QK_DIM + EPS)
        q_ref[0, hh] = (rq * (qa * q_cos + qb * q_sin) + qoff_ref[...]).astype(BF16)
        kn = kv[:, lo:lo + HEAD_PAD]
        ss = jnp.sum(kn * kn, axis=-1, keepdims=True) + ss_rope
        rk = lax.rsqrt(ss / QK_DIM + EPS)
        k_ref[0, hh] = (rk * (kn * gka + k_rot) + koff_ref[...]).astype(BF16)


def _inproj(x3, mixg, wa, cqg, ckvg, wq, wk, wvt, ga, gb, gka, gkb, q_off, k_off,
            cos, sin):
    b, s, _ = x3.shape
    tm = TM_PROJ
    hv = N_HEADS * V_DIM
    tok = lambda w: pl.BlockSpec((1, tm, w), lambda bi, i: (bi, i, 0))
    head = pl.BlockSpec((1, N_HEADS, tm, HEAD_PAD), lambda bi, i: (bi, 0, i, 0))
    vt_spec = pl.BlockSpec((1, 1, hv, tm), lambda bi, i: (bi, i, 0, 0))
    return pl.pallas_call(
        _inproj_body,
        name="inproj",
        grid=(b, s // tm),
        in_specs=[tok(D_MODEL), _const_spec(mixg.shape), _const_spec(wa.shape),
                  _const_spec(cqg.shape), _const_spec(ckvg.shape),
                  _const_spec(wq.shape), _const_spec(wk.shape), _const_spec(wvt.shape),
                  _const_spec(ga.shape), _const_spec(gb.shape),
                  _const_spec(gka.shape), _const_spec(gkb.shape),
                  _const_spec(q_off.shape), _const_spec(k_off.shape),
                  tok(HALF_ROPE), tok(HALF_ROPE)],
        out_specs=[tok(D_MODEL), tok(C_CONV), head, head, vt_spec],
        out_shape=[jax.ShapeDtypeStruct((b, s, D_MODEL), BF16),
                   jax.ShapeDtypeStruct((b, s, C_CONV), F32),
                   jax.ShapeDtypeStruct((b, N_HEADS, s, HEAD_PAD), BF16),
                   jax.ShapeDtypeStruct((b, N_HEADS, s, HEAD_PAD), BF16),
                   jax.ShapeDtypeStruct((b, s // tm, hv, tm), BF16)],
        compiler_params=pltpu.CompilerParams(
            dimension_semantics=("arbitrary", "arbitrary"),
            vmem_limit_bytes=VMEM_LIMIT),
    )(x3, mixg, wa, cqg, ckvg, wq, wk, wvt, ga, gb, gka, gkb, q_off, k_off, cos, sin)


def _pad_lanes(w, lo, total=HEAD_PAD):
    pad = [(0, 0)] * (w.ndim - 1) + [(lo, total - lo - w.shape[-1])]
    return jnp.pad(w, pad)


def _swap_halves(w):
    return jnp.concatenate([w[..., HALF_ROPE:], w[..., :HALF_ROPE]], axis=-1)


def _prep_mixer_weights(w_in, w_uq, w_ukv, q_norm, k_norm):
    o1 = 2 * C_CONV
    o3 = o1 + Q_LORA + KV_LORA
    o4 = o3 + ROPE_DIM
    w_rope = w_in[:, o3:o4]
    wa = jnp.concatenate(
        [w_in[:, :o3], _pad_lanes(w_rope, NOPE_DIM),
         _pad_lanes(_swap_halves(w_rope), NOPE_DIM)], axis=-1).astype(BF16)
    w_gates = w_in[:, o4:].astype(BF16)

    wq3 = w_uq.reshape(Q_LORA, N_HEADS, QK_DIM)
    wq_a = _pad_lanes(wq3, 0).reshape(Q_LORA, N_HEADS * HEAD_PAD)
    wq_b = _pad_lanes(_swap_halves(wq3[..., NOPE_DIM:]), NOPE_DIM)
    wq_b = wq_b.reshape(Q_LORA, N_HEADS * HEAD_PAD)
    wq = jnp.concatenate([wq_a, wq_b], axis=-1).astype(BF16)

    wkv3 = w_ukv.reshape(KV_LORA, N_HEADS, NOPE_DIM + V_DIM)
    wk = _pad_lanes(wkv3[..., :NOPE_DIM], 0).reshape(KV_LORA, N_HEADS * HEAD_PAD)
    wk = wk.astype(BF16)
    wvt = wkv3[..., NOPE_DIM:].reshape(KV_LORA, N_HEADS * V_DIM).T.astype(BF16)

    q_scale = QK_DIM ** -0.5 * math.log2(math.e)
    sign = jnp.concatenate([-jnp.ones((HALF_ROPE,), F32), jnp.ones((HALF_ROPE,), F32)])
    ga = _pad_lanes(q_norm * q_scale, 0).reshape(1, HEAD_PAD)
    gb = _pad_lanes(_swap_halves(q_norm[NOPE_DIM:]) * sign * q_scale, NOPE_DIM)
    gb = gb.reshape(1, HEAD_PAD)
    gka = _pad_lanes(k_norm, 0).reshape(1, HEAD_PAD)
    gkb = _pad_lanes(_swap_halves(k_norm[NOPE_DIM:]) * sign, NOPE_DIM).reshape(1, HEAD_PAD)

    bound = QK_DIM * q_scale * jnp.max(jnp.abs(q_norm)) * jnp.max(jnp.abs(k_norm))
    bounded = bound <= MAX_FIXED_OFFSET
    lane = jnp.arange(HEAD_PAD) == QK_DIM
    q_off = jnp.where(lane & bounded, -bound, 0.0).astype(F32).reshape(1, HEAD_PAD)
    k_off = jnp.where(lane, 1.0, 0.0).astype(F32).reshape(1, HEAD_PAD)
    return wa, w_gates, wq, wk, wvt, ga, gb, gka, gkb, q_off, k_off, bounded


def _attn_body(q_ref, k_ref, vt_ref, o_ref, m_ref, l_ref, acc_ref):
    i = pl.program_id(2)
    key_chunk = lax.broadcasted_iota(jnp.int32, (TK, TQ), 0) // CHUNK
    qry_chunk = lax.broadcasted_iota(jnp.int32, (TK, TQ), 1) // CHUNK
    diag_mask = key_chunk <= qry_chunk

    def scores_t(hh, j):
        kb = k_ref[0, hh, pl.ds(pl.multiple_of(j * TK, TK), TK), :]
        return lax.dot_general(kb, q_ref[0, hh], (((1,), (1,)), ((), ())),
                               preferred_element_type=F32)

    def values_t(hh, j):
        return vt_ref[0, j, hh * V_DIM:(hh + 1) * V_DIM, :]

    for hh in range(2):
        s = jnp.where(diag_mask, scores_t(hh, i), -jnp.inf)
        m = jnp.max(s, axis=0, keepdims=True)
        p = jnp.exp2(s - m)
        m_ref[hh] = m
        l_ref[hh] = jnp.sum(p, axis=0, keepdims=True)
        acc_ref[hh] = _dot(values_t(hh, i), p.astype(BF16))

    def step(j, carry):
        for hh in range(2):
            s = scores_t(hh, j)
            m_old = m_ref[hh]
            m_new = jnp.maximum(m_old, jnp.max(s, axis=0, keepdims=True))
            alpha = jnp.exp2(m_old - m_new)
            p = jnp.exp2(s - m_new)
            l_ref[hh] = alpha * l_ref[hh] + jnp.sum(p, axis=0, keepdims=True)
            acc_ref[hh] = alpha * acc_ref[hh] + _dot(values_t(hh, j), p.astype(BF16))
            m_ref[hh] = m_new
        return carry

    lax.fori_loop(0, i, step, 0)

    o_t = jnp.concatenate([acc_ref[0] / l_ref[0], acc_ref[1] / l_ref[1]], axis=0)
    o_ref[0] = o_t.T.astype(BF16)


def _attn_bounded_body(q_ref, k_ref, vt_ref, o_ref):
    qb = ATTN_Q_BLOCK
    n_tiles = q_ref.shape[2] // qb

    n_sub = qb // TK

    def diag_mask(nq):
        key_chunk = lax.broadcasted_iota(jnp.int32, (TK, nq), 0) // CHUNK
        qry_chunk = lax.broadcasted_iota(jnp.int32, (TK, nq), 1) // CHUNK
        return key_chunk <= qry_chunk

    masks = {qb - a * TK: diag_mask(qb - a * TK) for a in range(n_sub)}

    for i in range(n_tiles):
        q_lo = i * qb
        sums = [[None] * n_sub for _ in range(2)]
        accs = [[None] * n_sub for _ in range(2)]
        for hh in range(2):
            for j in range((q_lo + qb) // TK):
                first = max(0, j - q_lo // TK)
                nq = qb - first * TK
                q = q_ref[0, hh, q_lo + first * TK:q_lo + qb, :]
                kb = k_ref[0, hh, j * TK:(j + 1) * TK, :]
                p = jnp.exp2(lax.dot_general(kb, q, (((1,), (1,)), ((), ())),
                                             preferred_element_type=F32))
                if j * TK >= q_lo:
                    p = jnp.where(masks[nq], p, 0.0)
                ps = jnp.sum(p.reshape(TK // SUBLANES, SUBLANES, nq), axis=0)
                pv = _dot(vt_ref[0, j, hh * V_DIM:(hh + 1) * V_DIM, :], p.astype(BF16))
                for a in range(first, n_sub):
                    cols = slice((a - first) * TK, (a - first + 1) * TK)
                    sums[hh][a] = ps[:, cols] if sums[hh][a] is None else sums[hh][a] + ps[:, cols]
                    accs[hh][a] = pv[:, cols] if accs[hh][a] is None else accs[hh][a] + pv[:, cols]
        for a in range(n_sub):
            o_t = jnp.concatenate(
                [accs[hh][a] / jnp.sum(sums[hh][a], axis=0, keepdims=True) for hh in range(2)],
                axis=0)
            o_ref[0, q_lo + a * TK:q_lo + (a + 1) * TK, :] = o_t.T.astype(BF16)


def _attention_bounded(q, k, vt):
    b, _, s, _ = q.shape
    return pl.pallas_call(
        _attn_bounded_body,
        name="attn_bounded",
        grid=(b, N_HEADS // 2),
        in_specs=[pl.BlockSpec((1, 2, s, HEAD_PAD), lambda bi, p: (bi, p, 0, 0)),
                  pl.BlockSpec((1, 2, s, HEAD_PAD), lambda bi, p: (bi, p, 0, 0)),
                  pl.BlockSpec((1, s // TK, 2 * V_DIM, TK), lambda bi, p: (bi, 0, p, 0))],
        out_specs=pl.BlockSpec((1, s, 2 * V_DIM), lambda bi, p: (bi, 0, p)),
        out_shape=jax.ShapeDtypeStruct((b, s, N_HEADS * V_DIM), BF16),
        compiler_params=pltpu.CompilerParams(
            dimension_semantics=("arbitrary", "arbitrary"),
            vmem_limit_bytes=VMEM_LIMIT),
    )(q, k, vt)


def _attention(q, k, vt):
    b, _, s, _ = q.shape
    assert TQ == TK
    scratch = [pltpu.VMEM((2, 1, TQ), F32), pltpu.VMEM((2, 1, TQ), F32),
               pltpu.VMEM((2, V_DIM, TQ), F32)]
    return pl.pallas_call(
        _attn_body,
        name="attn",
        grid=(b, N_HEADS // 2, s // TQ),
        in_specs=[pl.BlockSpec((1, 2, TQ, HEAD_PAD), lambda bi, p, i: (bi, p, i, 0)),
                  pl.BlockSpec((1, 2, s, HEAD_PAD), lambda bi, p, i: (bi, p, 0, 0)),
                  pl.BlockSpec((1, s // TK, 2 * V_DIM, TK), lambda bi, p, i: (bi, 0, p, 0))],
        out_specs=pl.BlockSpec((1, TQ, 2 * V_DIM), lambda bi, p, i: (bi, i, p)),
        out_shape=jax.ShapeDtypeStruct((b, s, N_HEADS * V_DIM), BF16),
        scratch_shapes=scratch,
        compiler_params=pltpu.CompilerParams(
            dimension_semantics=("arbitrary", "arbitrary", "arbitrary"),
            vmem_limit_bytes=VMEM_LIMIT),
    )(q, k, vt)


def _conv_fill(u, halo, ubuf_ref, shift_ref):
    ubuf_ref[0:HALO, :] = halo
    ubuf_ref[HALO:HALO + TM_MIX, :] = u
    for bb in range(1, SUBLANES):
        shift_ref[bb - 1] = ubuf_ref[bb:bb + SHIFT_ROWS, :]


def _conv_branch(row0, nrows, cw_ref, cb_ref, lng_ref, lnb_ref, ubuf_ref, shift_ref, conv_ref):
    first = HALO - (CONV_WIDTH - 1)
    for cb in range(C_CONV // LANES):
        cols = slice(cb * LANES, (cb + 1) * LANES)
        for rb in range(row0 // CONV_ROWS, (row0 + nrows) // CONV_ROWS):
            acc = None
            for kk in range(CONV_WIDTH):
                aa, bb = divmod(first + kk, SUBLANES)
                r0 = aa * SUBLANES + rb * CONV_ROWS
                if bb == 0:
                    win = ubuf_ref[r0:r0 + CONV_ROWS, cols]
                else:
                    win = shift_ref[bb - 1, r0:r0 + CONV_ROWS, cols]
                term = cw_ref[kk:kk + 1, cols] * win
                acc = term if acc is None else acc + term
            conv_ref[rb * CONV_ROWS:(rb + 1) * CONV_ROWS, cols] = acc + cb_ref[:, cols]
    conv = conv_ref[row0:row0 + nrows, :]
    mu = jnp.mean(conv, axis=-1, keepdims=True)
    cc = conv - mu
    ln = cc * lax.rsqrt(jnp.mean(cc * cc, axis=-1, keepdims=True) + EPS)
    ln = ln * lng_ref[...] + lnb_ref[...]
    return (ln * jax.nn.sigmoid(ln)).astype(BF16)


def _mix_body(x_ref, h_ref, u_ref, halo_ref, o_ref, wg_ref, gbias_ref,
              cw_ref, cb_ref, lng_ref, lnb_ref, wco_ref, wmo_ref, wout_ref,
              out_ref, ubuf_ref, shift_ref, conv_ref):
    i = pl.program_id(1)
    halo = halo_ref[0]
    halo = jnp.where(i > 0, halo, jnp.zeros_like(halo))
    _conv_fill(u_ref[0], halo, ubuf_ref, shift_ref)
    act = _conv_branch(0, TM_MIX, cw_ref, cb_ref, lng_ref, lnb_ref,
                       ubuf_ref, shift_ref, conv_ref)
    gates = jax.nn.sigmoid(_dot(h_ref[0], wg_ref[...]) + gbias_ref[...])
    y_conv = _dot(act, wco_ref[...])
    y_mla = _dot(o_ref[0], wmo_ref[...])
    y = gates[:, :D_MODEL] * y_conv + gates[:, D_MODEL:] * y_mla
    out_ref[0] = x_ref[0] + _dot(y.astype(BF16), wout_ref[...])


def _mix(x3, h, u, o, w_gates, gbias, cw, cb, lng, lnb, wco, wmo, wout):
    b, s, _ = x3.shape
    tm = TM_MIX
    tok = lambda w: pl.BlockSpec((1, tm, w), lambda bi, i: (bi, i, 0))
    halo = pl.BlockSpec(
        (1, HALO, C_CONV), lambda bi, i: (bi, jnp.maximum(i * (tm // HALO) - 1, 0), 0))
    consts = [w_gates, gbias, cw, cb, lng, lnb, wco, wmo, wout]
    return pl.pallas_call(
        _mix_body,
        name="mix",
        grid=(b, s // tm),
        in_specs=[tok(D_MODEL), tok(D_MODEL), tok(C_CONV), halo, tok(N_HEADS * V_DIM)]
                 + [_const_spec(c.shape) for c in consts],
        out_specs=tok(D_MODEL),
        out_shape=jax.ShapeDtypeStruct((b, s, D_MODEL), F32),
        scratch_shapes=[pltpu.VMEM((HALO + tm, C_CONV), F32),
                        pltpu.VMEM((SUBLANES - 1, SHIFT_ROWS, C_CONV), F32),
                        pltpu.VMEM((tm, C_CONV), F32)],
        compiler_params=pltpu.CompilerParams(
            dimension_semantics=("arbitrary", "arbitrary"),
            vmem_limit_bytes=VMEM_LIMIT),
    )(x3, h, u, u, o, *consts)


def kernel(x, positions, ffn1_norm, ffn1_w_gate, ffn1_w_up, ffn1_w_down, mix_norm, w_in,
           gate_bias, conv_w, conv_b, conv_ln_g, conv_ln_b, w_conv_out, cq_norm, ckv_norm,
           w_uq, w_ukv, q_norm, k_norm, w_mla_out, w_out, ffn2_norm, ffn2_w_gate, ffn2_w_up,
           ffn2_w_down):
    b, s, d = x.shape
    depth = ffn1_norm.shape[0]
    assert d == D_MODEL and s % max(TM_PROJ, TM_MIX, TQ, ATTN_Q_BLOCK) == 0
    assert (b * s) % TM_FFN == 0 and ATTN_Q_BLOCK % TK == 0
    row = lambda v: v.reshape(1, -1)
    cos, sin = _rope_tables(positions)
    ffn1_w = [w.astype(BF16) for w in (ffn1_w_gate, ffn1_w_up, ffn1_w_down)]
    ffn2_w = [w.astype(BF16) for w in (ffn2_w_gate, ffn2_w_up, ffn2_w_down)]
    for l in range(depth):
        x = _ffn(x.reshape(b * s, d), row(ffn1_norm[l]), *ffn1_w, l).reshape(b, s, d)

        wa, w_gates, wq, wk, wvt, ga, gb, gka, gkb, q_off, k_off, bounded = _prep_mixer_weights(
            w_in[l], w_uq[l], w_ukv[l], q_norm[l], k_norm[l])
        h, u, q, k, vt = _inproj(x, row(mix_norm[l]), wa, row(cq_norm[l]), row(ckv_norm[l]),
                                 wq, wk, wvt, ga, gb, gka, gkb, q_off, k_off, cos, sin)
        o = lax.cond(bounded, _attention_bounded, _attention, q, k, vt)
        x = _mix(x, h, u, o, w_gates, gate_bias[l].reshape(1, 2 * D_MODEL),
                 conv_w[l], row(conv_b[l]), row(conv_ln_g[l]), row(conv_ln_b[l]),
                 w_conv_out[l].astype(BF16), w_mla_out[l].astype(BF16),
                 w_out[l].astype(BF16))

        x = _ffn(x.reshape(b * s, d), row(ffn2_norm[l]), *ffn2_w, l).reshape(b, s, d)
    return x
```

```python
import math

import jax
import jax.numpy as jnp
from jax import lax
from jax.experimental import pallas as pl
from jax.experimental.pallas import tpu as pltpu

D_MODEL = 1024
D_FF = 2816
C_CONV = 512
CONV_WIDTH = 31
N_HEADS = 8
NOPE_DIM = 64
ROPE_DIM = 32
QK_DIM = NOPE_DIM + ROPE_DIM
V_DIM = 64
Q_LORA = 256
KV_LORA = 256
CHUNK = 64
ROPE_THETA = 10000.0
EPS = 1e-6

LANES = 128
SUBLANES = 8
MAX_FIXED_OFFSET = 50.0
HEAD_PAD = LANES
HALF_ROPE = ROPE_DIM // 2
VMEM_LIMIT = 56 * 1024 * 1024
FF_CHUNK = 256
N_FF_CHUNKS = D_FF // FF_CHUNK
TM_FFN = 1024
TM_PROJ = 512
TM_MIX = 512
TQ = 512
TK = TM_PROJ
ATTN_Q_BLOCK = 2048
HALO = 32
SHIFT_ROWS = TM_MIX + HALO - SUBLANES
CONV_ROWS = 64
F32 = jnp.float32
BF16 = jnp.bfloat16


def _const_spec(shape):
    zeros = (0,) * len(shape)
    return pl.BlockSpec(shape, lambda *_: zeros, pipeline_mode=pl.Buffered(1))


def _rms(x, g):
    ms = jnp.mean(x * x, axis=-1, keepdims=True)
    return x * lax.rsqrt(ms + EPS) * g


def _dot(a, b):
    return jnp.dot(a, b, preferred_element_type=F32)


def _rope_body(pos_ref, inv_ref, cos_ref, sin_ref):
    ang = pos_ref[...].astype(F32) * inv_ref[...]
    cos_ref[...] = jnp.cos(ang)
    sin_ref[...] = jnp.sin(ang)


def _rope_tables(positions):
    b, s = positions.shape
    t = b * s
    rows = t * HALF_ROPE // LANES
    inv_freq = ROPE_THETA ** (-jnp.arange(0, ROPE_DIM, 2, dtype=F32) / ROPE_DIM)
    pos_rep = jnp.broadcast_to(positions.reshape(t, 1), (t, HALF_ROPE)).reshape(rows, LANES)
    inv_rep = jnp.tile(inv_freq, LANES // HALF_ROPE).reshape(1, LANES)
    tr = 512
    cos, sin = pl.pallas_call(
        _rope_body,
        name="rope_tables",
        grid=(rows // tr,),
        in_specs=[pl.BlockSpec((tr, LANES), lambda i: (i, 0)),
                  pl.BlockSpec((1, LANES), lambda i: (0, 0))],
        out_specs=[pl.BlockSpec((tr, LANES), lambda i: (i, 0))] * 2,
        out_shape=[jax.ShapeDtypeStruct((rows, LANES), F32)] * 2,
    )(pos_rep, inv_rep)
    return cos.reshape(b, s, HALF_ROPE), sin.reshape(b, s, HALF_ROPE)


def _ffn_value(x, g, wg_ref, wu_ref, wd_ref):
    xn = _rms(x, g).astype(BF16)
    acc = None
    for c in range(N_FF_CHUNKS):
        cols = slice(c * FF_CHUNK, (c + 1) * FF_CHUNK)
        gate = _dot(xn, wg_ref[0, :, cols].astype(BF16))
        up = _dot(xn, wu_ref[0, :, cols].astype(BF16))
        a = (gate * jax.nn.sigmoid(gate) * up).astype(BF16)
        d = _dot(a, wd_ref[0, cols, :].astype(BF16))
        acc = d if acc is None else acc + d
    return x + 0.5 * acc


def _ffn_body(x_ref, g_ref, wg_ref, wu_ref, wd_ref, o_ref):
    o_ref[...] = _ffn_value(x_ref[...], g_ref[...], wg_ref, wu_ref, wd_ref)


def _ffn(x2, g, w_gate, w_up, w_down, layer):
    t = x2.shape[0]
    tok = pl.BlockSpec((TM_FFN, D_MODEL), lambda i: (i, 0))
    layer_spec = lambda w: pl.BlockSpec((1,) + w.shape[1:], lambda i: (layer, 0, 0),
                                        pipeline_mode=pl.Buffered(1))
    return pl.pallas_call(
        _ffn_body,
        name="ffn",
        grid=(t // TM_FFN,),
        in_specs=[tok, _const_spec((1, D_MODEL)),
                  layer_spec(w_gate), layer_spec(w_up), layer_spec(w_down)],
        out_specs=tok,
        out_shape=jax.ShapeDtypeStruct((t, D_MODEL), F32),
        compiler_params=pltpu.CompilerParams(
            dimension_semantics=("arbitrary",), vmem_limit_bytes=VMEM_LIMIT),
    )(x2, g, w_gate, w_up, w_down)


def _inproj_body(x_ref, mixg_ref, wa_ref, cqg_ref, ckvg_ref, wq_ref, wk_ref, wvt_ref,
                 ga_ref, gb_ref, gka_ref, gkb_ref, qoff_ref, koff_ref, ca_ref, sb_ref,
                 h_ref, u_ref, q_ref, k_ref, vt_ref):
    h = _rms(x_ref[0], mixg_ref[...]).astype(BF16)
    h_ref[0] = h
    proj = _dot(h, wa_ref[...])
    o1 = 2 * C_CONV
    o2 = o1 + Q_LORA
    o3 = o2 + KV_LORA
    u_ref[0] = proj[:, :C_CONV] * jax.nn.sigmoid(proj[:, C_CONV:o1])
    cqn = _rms(proj[:, o1:o2], cqg_ref[...]).astype(BF16)
    ckvn = _rms(proj[:, o2:o3], ckvg_ref[...]).astype(BF16)
    kra = proj[:, o3:o3 + HEAD_PAD]
    krb = proj[:, o3 + HEAD_PAD:o3 + 2 * HEAD_PAD]
    qq = _dot(cqn, wq_ref[...])
    kv = _dot(ckvn, wk_ref[...])
    vt_ref[0, 0] = lax.dot_general(wvt_ref[...], ckvn, (((1,), (1,)), ((), ())),
                                   preferred_element_type=F32).astype(BF16)
    cos = ca_ref[0]
    sin = sb_ref[0]
    tm = cos.shape[0]
    ca = jnp.concatenate([jnp.ones((tm, NOPE_DIM), F32), cos, cos,
                          jnp.zeros((tm, HEAD_PAD - QK_DIM), F32)], axis=1)
    sb = jnp.concatenate([jnp.zeros((tm, NOPE_DIM), F32), sin, sin,
                          jnp.zeros((tm, HEAD_PAD - QK_DIM), F32)], axis=1)
    q_cos = ga_ref[...] * ca
    q_sin = gb_ref[...] * sb
    gka = gka_ref[...]
    k_rot = kra * (gka * ca) + krb * (gkb_ref[...] * sb)
    ss_rope = jnp.sum(kra * kra, axis=-1, keepdims=True)
    hw = N_HEADS * HEAD_PAD
    for hh in range(N_HEADS):
        lo = hh * HEAD_PAD
        qa = qq[:, lo:lo + HEAD_PAD]
        qb = qq[:, hw + lo:hw + lo + HEAD_PAD]
        rq = lax.rsqrt(jnp.sum(qa * qa, axis=-1, keepdims=True) / QK_DIM + EPS)
        q_ref[0, hh] = (rq * (qa * q_cos + qb * q_sin) + qoff_ref[...]).astype(BF16)
        kn = kv[:, lo:lo + HEAD_PAD]
        ss = jnp.sum(kn * kn, axis=-1, keepdims=True) + ss_rope
        rk = lax.rsqrt(ss / QK_DIM + EPS)
        k_ref[0, hh] = (rk * (kn * gka + k_rot) + koff_ref[...]).astype(BF16)


def _inproj(x3, mixg, wa, cqg, ckvg, wq, wk, wvt, ga, gb, gka, gkb, q_off, k_off,
            cos, sin):
    b, s, _ = x3.shape
    tm = TM_PROJ
    hv = N_HEADS * V_DIM
    tok = lambda w: pl.BlockSpec((1, tm, w), lambda bi, i: (bi, i, 0))
    head = pl.BlockSpec((1, N_HEADS, tm, HEAD_PAD), lambda bi, i: (bi, 0, i, 0))
    vt_spec = pl.BlockSpec((1, 1, hv, tm), lambda bi, i: (bi, i, 0, 0))
    return pl.pallas_call(
        _inproj_body,
        name="inproj",
        grid=(b, s // tm),
        in_specs=[tok(D_MODEL), _const_spec(mixg.shape), _const_spec(wa.shape),
                  _const_spec(cqg.shape), _const_spec(ckvg.shape),
                  _const_spec(wq.shape), _const_spec(wk.shape), _const_spec(wvt.shape),
                  _const_spec(ga.shape), _const_spec(gb.shape),
                  _const_spec(gka.shape), _const_spec(gkb.shape),
                  _const_spec(q_off.shape), _const_spec(k_off.shape),
                  tok(HALF_ROPE), tok(HALF_ROPE)],
        out_specs=[tok(D_MODEL), tok(C_CONV), head, head, vt_spec],
        out_shape=[jax.ShapeDtypeStruct((b, s, D_MODEL), BF16),
                   jax.ShapeDtypeStruct((b, s, C_CONV), F32),
                   jax.ShapeDtypeStruct((b, N_HEADS, s, HEAD_PAD), BF16),
                   jax.ShapeDtypeStruct((b, N_HEADS, s, HEAD_PAD), BF16),
                   jax.ShapeDtypeStruct((b, s // tm, hv, tm), BF16)],
        compiler_params=pltpu.CompilerParams(
            dimension_semantics=("arbitrary", "arbitrary"),
            vmem_limit_bytes=VMEM_LIMIT),
    )(x3, mixg, wa, cqg, ckvg, wq, wk, wvt, ga, gb, gka, gkb, q_off, k_off, cos, sin)


def _pad_lanes(w, lo, total=HEAD_PAD):
    pad = [(0, 0)] * (w.ndim - 1) + [(lo, total - lo - w.shape[-1])]
    return jnp.pad(w, pad)


def _swap_halves(w):
    return jnp.concatenate([w[..., HALF_ROPE:], w[..., :HALF_ROPE]], axis=-1)


def _prep_mixer_weights(w_in, w_uq, w_ukv, q_norm, k_norm):
    o1 = 2 * C_CONV
    o3 = o1 + Q_LORA + KV_LORA
    o4 = o3 + ROPE_DIM
    w_rope = w_in[:, o3:o4]
    wa = jnp.concatenate(
        [w_in[:, :o3], _pad_lanes(w_rope, NOPE_DIM),
         _pad_lanes(_swap_halves(w_rope), NOPE_DIM)], axis=-1).astype(BF16)
    w_gates = w_in[:, o4:].astype(BF16)

    wq3 = w_uq.reshape(Q_LORA, N_HEADS, QK_DIM)
    wq_a = _pad_lanes(wq3, 0).reshape(Q_LORA, N_HEADS * HEAD_PAD)
    wq_b = _pad_lanes(_swap_halves(wq3[..., NOPE_DIM:]), NOPE_DIM)
    wq_b = wq_b.reshape(Q_LORA, N_HEADS * HEAD_PAD)
    wq = jnp.concatenate([wq_a, wq_b], axis=-1).astype(BF16)

    wkv3 = w_ukv.reshape(KV_LORA, N_HEADS, NOPE_DIM + V_DIM)
    wk = _pad_lanes(wkv3[..., :NOPE_DIM], 0).reshape(KV_LORA, N_HEADS * HEAD_PAD)
    wk = wk.astype(BF16)
    wvt = wkv3[..., NOPE_DIM:].reshape(KV_LORA, N_HEADS * V_DIM).T.astype(BF16)

    q_scale = QK_DIM ** -0.5 * math.log2(math.e)
    sign = jnp.concatenate([-jnp.ones((HALF_ROPE,), F32), jnp.ones((HALF_ROPE,), F32)])
    ga = _pad_lanes(q_norm * q_scale, 0).reshape(1, HEAD_PAD)
    gb = _pad_lanes(_swap_halves(q_norm[NOPE_DIM:]) * sign * q_scale, NOPE_DIM)
    gb = gb.reshape(1, HEAD_PAD)
    gka = _pad_lanes(k_norm, 0).reshape(1, HEAD_PAD)
    gkb = _pad_lanes(_swap_halves(k_norm[NOPE_DIM:]) * sign, NOPE_DIM).reshape(1, HEAD_PAD)

    bound = QK_DIM * q_scale * jnp.max(jnp.abs(q_norm)) * jnp.max(jnp.abs(k_norm))
    bounded = bound <= MAX_FIXED_OFFSET
    lane = jnp.arange(HEAD_PAD) == QK_DIM
    q_off = jnp.where(lane & bounded, -bound, 0.0).astype(F32).reshape(1, HEAD_PAD)
    k_off = jnp.where(lane, 1.0, 0.0).astype(F32).reshape(1, HEAD_PAD)
    return wa, w_gates, wq, wk, wvt, ga, gb, gka, gkb, q_off, k_off, bounded


def _attn_body(q_ref, k_ref, vt_ref, o_ref, m_ref, l_ref, acc_ref):
    i = pl.program_id(2)
    key_chunk = lax.broadcasted_iota(jnp.int32, (TK, TQ), 0) // CHUNK
    qry_chunk = lax.broadcasted_iota(jnp.int32, (TK, TQ), 1) // CHUNK
    diag_mask = key_chunk <= qry_chunk

    def scores_t(hh, j):
        kb = k_ref[0, hh, pl.ds(pl.multiple_of(j * TK, TK), TK), :]
        return lax.dot_general(kb, q_ref[0, hh], (((1,), (1,)), ((), ())),
                               preferred_element_type=F32)

    def values_t(hh, j):
        return vt_ref[0, j, hh * V_DIM:(hh + 1) * V_DIM, :]

    for hh in range(2):
        s = jnp.where(diag_mask, scores_t(hh, i), -jnp.inf)
        m = jnp.max(s, axis=0, keepdims=True)
        p = jnp.exp2(s - m)
        m_ref[hh] = m
        l_ref[hh] = jnp.sum(p, axis=0, keepdims=True)
        acc_ref[hh] = _dot(values_t(hh, i), p.astype(BF16))

    def step(j, carry):
        for hh in range(2):
            s = scores_t(hh, j)
            m_old = m_ref[hh]
            m_new = jnp.maximum(m_old, jnp.max(s, axis=0, keepdims=True))
            alpha = jnp.exp2(m_old - m_new)
            p = jnp.exp2(s - m_new)
            l_ref[hh] = alpha * l_ref[hh] + jnp.sum(p, axis=0, keepdims=True)
            acc_ref[hh] = alpha * acc_ref[hh] + _dot(values_t(hh, j), p.astype(BF16))
            m_ref[hh] = m_new
        return carry

    lax.fori_loop(0, i, step, 0)

    o_t = jnp.concatenate([acc_ref[0] / l_ref[0], acc_ref[1] / l_ref[1]], axis=0)
    o_ref[0] = o_t.T.astype(BF16)


def _attn_bounded_body(q_ref, k_ref, vt_ref, o_ref):
    qb = ATTN_Q_BLOCK
    n_tiles = q_ref.shape[2] // qb

    n_sub = qb // TK

    def diag_mask(nq):
        key_chunk = lax.broadcasted_iota(jnp.int32, (TK, nq), 0) // CHUNK
        qry_chunk = lax.broadcasted_iota(jnp.int32, (TK, nq), 1) // CHUNK
        return key_chunk <= qry_chunk

    masks = {qb - a * TK: diag_mask(qb - a * TK) for a in range(n_sub)}

    for i in range(n_tiles):
        q_lo = i * qb
        sums = [[None] * n_sub for _ in range(2)]
        accs = [[None] * n_sub for _ in range(2)]
        for hh in range(2):
            for j in range((q_lo + qb) // TK):
                first = max(0, j - q_lo // TK)
                nq = qb - first * TK
                q = q_ref[0, hh, q_lo + first * TK:q_lo + qb, :]
                kb = k_ref[0, hh, j * TK:(j + 1) * TK, :]
                p = jnp.exp2(lax.dot_general(kb, q, (((1,), (1,)), ((), ())),
                                             preferred_element_type=F32))
                if j * TK >= q_lo:
                    p = jnp.where(masks[nq], p, 0.0)
                ps = jnp.sum(p.reshape(TK // SUBLANES, SUBLANES, nq), axis=0)
                pv = _dot(vt_ref[0, j, hh * V_DIM:(hh + 1) * V_DIM, :], p.astype(BF16))
                for a in range(first, n_sub):
                    cols = slice((a - first) * TK, (a - first + 1) * TK)
                    sums[hh][a] = ps[:, cols] if sums[hh][a] is None else sums[hh][a] + ps[:, cols]
                    accs[hh][a] = pv[:, cols] if accs[hh][a] is None else accs[hh][a] + pv[:, cols]
        for a in range(n_sub):
            o_t = jnp.concatenate(
                [accs[hh][a] / jnp.sum(sums[hh][a], axis=0, keepdims=True) for hh in range(2)],
                axis=0)
            o_ref[0, q_lo + a * TK:q_lo + (a + 1) * TK, :] = o_t.T.astype(BF16)


def _attention_bounded(q, k, vt):
    b, _, s, _ = q.shape
    return pl.pallas_call(
        _attn_bounded_body,
        name="attn_bounded",
        grid=(b, N_HEADS // 2),
        in_specs=[pl.BlockSpec((1, 2, s, HEAD_PAD), lambda bi, p: (bi, p, 0, 0)),
                  pl.BlockSpec((1, 2, s, HEAD_PAD), lambda bi, p: (bi, p, 0, 0)),
                  pl.BlockSpec((1, s // TK, 2 * V_DIM, TK), lambda bi, p: (bi, 0, p, 0))],
        out_specs=pl.BlockSpec((1, s, 2 * V_DIM), lambda bi, p: (bi, 0, p)),
        out_shape=jax.ShapeDtypeStruct((b, s, N_HEADS * V_DIM), BF16),
        compiler_params=pltpu.CompilerParams(
            dimension_semantics=("arbitrary", "arbitrary"),
            vmem_limit_bytes=VMEM_LIMIT),
    )(q, k, vt)


def _attention(q, k, vt):
    b, _, s, _ = q.shape
    assert TQ == TK
    scratch = [pltpu.VMEM((2, 1, TQ), F32), pltpu.VMEM((2, 1, TQ), F32),
               pltpu.VMEM((2, V_DIM, TQ), F32)]
    return pl.pallas_call(
        _attn_body,
        name="attn",
        grid=(b, N_HEADS // 2, s // TQ),
        in_specs=[pl.BlockSpec((1, 2, TQ, HEAD_PAD), lambda bi, p, i: (bi, p, i, 0)),
                  pl.BlockSpec((1, 2, s, HEAD_PAD), lambda bi, p, i: (bi, p, 0, 0)),
                  pl.BlockSpec((1, s // TK, 2 * V_DIM, TK), lambda bi, p, i: (bi, 0, p, 0))],
        out_specs=pl.BlockSpec((1, TQ, 2 * V_DIM), lambda bi, p, i: (bi, i, p)),
        out_shape=jax.ShapeDtypeStruct((b, s, N_HEADS * V_DIM), BF16),
        scratch_shapes=scratch,
        compiler_params=pltpu.CompilerParams(
            dimension_semantics=("arbitrary", "arbitrary", "arbitrary"),
            vmem_limit_bytes=VMEM_LIMIT),
    )(q, k, vt)


def _conv_fill(u, halo, ubuf_ref, shift_ref):
    ubuf_ref[0:HALO, :] = halo
    ubuf_ref[HALO:HALO + TM_MIX, :] = u
    for bb in range(1, SUBLANES):
        shift_ref[bb - 1] = ubuf_ref[bb:bb + SHIFT_ROWS, :]


def _conv_branch(row0, nrows, cw_ref, cb_ref, lng_ref, lnb_ref, ubuf_ref, shift_ref, conv_ref):
    first = HALO - (CONV_WIDTH - 1)
    for cb in range(C_CONV // LANES):
        cols = slice(cb * LANES, (cb + 1) * LANES)
        for rb in range(row0 // CONV_ROWS, (row0 + nrows) // CONV_ROWS):
            acc = None
            for kk in range(CONV_WIDTH):
                aa, bb = divmod(first + kk, SUBLANES)
                r0 = aa * SUBLANES + rb * CONV_ROWS
                if bb == 0:
                    win = ubuf_ref[r0:r0 + CONV_ROWS, cols]
                else:
                    win = shift_ref[bb - 1, r0:r0 + CONV_ROWS, cols]
                term = cw_ref[kk:kk + 1, cols] * win
                acc = term if acc is None else acc + term
            conv_ref[rb * CONV_ROWS:(rb + 1) * CONV_ROWS, cols] = acc + cb_ref[:, cols]
    conv = conv_ref[row0:row0 + nrows, :]
    mu = jnp.mean(conv, axis=-1, keepdims=True)
    cc = conv - mu
    ln = cc * lax.rsqrt(jnp.mean(cc * cc, axis=-1, keepdims=True) + EPS)
    ln = ln * lng_ref[...] + lnb_ref[...]
    return (ln * jax.nn.sigmoid(ln)).astype(BF16)


def _mix_body(x_ref, h_ref, u_ref, halo_ref, o_ref, wg_ref, gbias_ref,
              cw_ref, cb_ref, lng_ref, lnb_ref, wco_ref, wmo_ref, wout_ref,
              out_ref, ubuf_ref, shift_ref, conv_ref):
    i = pl.program_id(1)
    halo = halo_ref[0]
    halo = jnp.where(i > 0, halo, jnp.zeros_like(halo))
    _conv_fill(u_ref[0], halo, ubuf_ref, shift_ref)
    act = _conv_branch(0, TM_MIX, cw_ref, cb_ref, lng_ref, lnb_ref,
                       ubuf_ref, shift_ref, conv_ref)
    gates = jax.nn.sigmoid(_dot(h_ref[0], wg_ref[...]) + gbias_ref[...])
    y_conv = _dot(act, wco_ref[...])
    y_mla = _dot(o_ref[0], wmo_ref[...])
    y = gates[:, :D_MODEL] * y_conv + gates[:, D_MODEL:] * y_mla
    out_ref[0] = x_ref[0] + _dot(y.astype(BF16), wout_ref[...])


def _mix(x3, h, u, o, w_gates, gbias, cw, cb, lng, lnb, wco, wmo, wout):
    b, s, _ = x3.shape
    tm = TM_MIX
    tok = lambda w: pl.BlockSpec((1, tm, w), lambda bi, i: (bi, i, 0))
    halo = pl.BlockSpec(
        (1, HALO, C_CONV), lambda bi, i: (bi, jnp.maximum(i * (tm // HALO) - 1, 0), 0))
    consts = [w_gates, gbias, cw, cb, lng, lnb, wco, wmo, wout]
    return pl.pallas_call(
        _mix_body,
        name="mix",
        grid=(b, s // tm),
        in_specs=[tok(D_MODEL), tok(D_MODEL), tok(C_CONV), halo, tok(N_HEADS * V_DIM)]
                 + [_const_spec(c.shape) for c in consts],
        out_specs=tok(D_MODEL),
        out_shape=jax.ShapeDtypeStruct((b, s, D_MODEL), F32),
        scratch_shapes=[pltpu.VMEM((HALO + tm, C_CONV), F32),
                        pltpu.VMEM((SUBLANES - 1, SHIFT_ROWS, C_CONV), F32),
                        pltpu.VMEM((tm, C_CONV), F32)],
        compiler_params=pltpu.CompilerParams(
            dimension_semantics=("arbitrary", "arbitrary"),
            vmem_limit_bytes=VMEM_LIMIT),
    )(x3, h, u, u, o, *consts)


def kernel(x, positions, ffn1_norm, ffn1_w_gate, ffn1_w_up, ffn1_w_down, mix_norm, w_in,
           gate_bias, conv_w, conv_b, conv_ln_g, conv_ln_b, w_conv_out, cq_norm, ckv_norm,
           w_uq, w_ukv, q_norm, k_norm, w_mla_out, w_out, ffn2_norm, ffn2_w_gate, ffn2_w_up,
           ffn2_w_down):
    b, s, d = x.shape
    depth = ffn1_norm.shape[0]
    assert d == D_MODEL and s % max(TM_PROJ, TM_MIX, TQ, ATTN_Q_BLOCK) == 0
    assert (b * s) % TM_FFN == 0 and ATTN_Q_BLOCK % TK == 0
    row = lambda v: v.reshape(1, -1)
    cos, sin = _rope_tables(positions)
    ffn1_w = (ffn1_w_gate, ffn1_w_up, ffn1_w_down)
    ffn2_w = (ffn2_w_gate, ffn2_w_up, ffn2_w_down)
    for l in range(depth):
        x = _ffn(x.reshape(b * s, d), row(ffn1_norm[l]), *ffn1_w, l).reshape(b, s, d)

        wa, w_gates, wq, wk, wvt, ga, gb, gka, gkb, q_off, k_off, bounded = _prep_mixer_weights(
            w_in[l], w_uq[l], w_ukv[l], q_norm[l], k_norm[l])
        h, u, q, k, vt = _inproj(x, row(mix_norm[l]), wa, row(cq_norm[l]), row(ckv_norm[l]),
                                 wq, wk, wvt, ga, gb, gka, gkb, q_off, k_off, cos, sin)
        o = lax.cond(bounded, _attention_bounded, _attention, q, k, vt)
        x = _mix(x, h, u, o, w_gates, gate_bias[l].reshape(1, 2 * D_MODEL),
                 conv_w[l], row(conv_b[l]), row(conv_ln_g[l]), row(conv_ln_b[l]),
                 w_conv_out[l].astype(BF16), w_mla_out[l].astype(BF16),
                 w_out[l].astype(BF16))

        x = _ffn(x.reshape(b * s, d), row(ffn2_norm[l]), *ffn2_w, l).reshape(b, s, d)
    return x
```

```python
import math

import jax
import jax.numpy as jnp
from jax import lax
from jax.experimental import pallas as pl
from jax.experimental.pallas import tpu as pltpu

D_MODEL = 1024
D_FF = 2816
C_CONV = 512
CONV_WIDTH = 31
N_HEADS = 8
NOPE_DIM = 64
ROPE_DIM = 32
QK_DIM = NOPE_DIM + ROPE_DIM
V_DIM = 64
Q_LORA = 256
KV_LORA = 256
CHUNK = 64
ROPE_THETA = 10000.0
EPS = 1e-6

LANES = 128
SUBLANES = 8
MAX_FIXED_OFFSET = 50.0
HEAD_PAD = LANES
HALF_ROPE = ROPE_DIM // 2
VMEM_LIMIT = 56 * 1024 * 1024
FF_CHUNK = 256
N_FF_CHUNKS = D_FF // FF_CHUNK
TM_FFN = 1024
TM_PROJ = 512
TM_MIX = 512
TQ = 512
TK = TM_PROJ
ATTN_Q_BLOCK = 2048
HALO = 32
SHIFT_ROWS = TM_MIX + HALO - SUBLANES
CONV_ROWS = 64
F32 = jnp.float32
BF16 = jnp.bfloat16


def _const_spec(shape):
    zeros = (0,) * len(shape)
    return pl.BlockSpec(shape, lambda *_: zeros, pipeline_mode=pl.Buffered(1))


def _rms(x, g):
    ms = jnp.mean(x * x, axis=-1, keepdims=True)
    return x * lax.rsqrt(ms + EPS) * g


def _dot(a, b):
    return jnp.dot(a, b, preferred_element_type=F32)


def _rope_body(pos_ref, inv_ref, cos_ref, sin_ref):
    ang = pos_ref[...].astype(F32) * inv_ref[...]
    cos_ref[...] = jnp.cos(ang)
    sin_ref[...] = jnp.sin(ang)


def _rope_tables(positions):
    b, s = positions.shape
    t = b * s
    rows = t * HALF_ROPE // LANES
    inv_freq = ROPE_THETA ** (-jnp.arange(0, ROPE_DIM, 2, dtype=F32) / ROPE_DIM)
    pos_rep = jnp.broadcast_to(positions.reshape(t, 1), (t, HALF_ROPE)).reshape(rows, LANES)
    inv_rep = jnp.tile(inv_freq, LANES // HALF_ROPE).reshape(1, LANES)
    tr = min(rows, TM_PROJ)
    cos, sin = pl.pallas_call(
        _rope_body,
        name="rope_tables",
        grid=(rows // tr,),
        in_specs=[pl.BlockSpec((tr, LANES), lambda i: (i, 0)),
                  pl.BlockSpec((1, LANES), lambda i: (0, 0))],
        out_specs=[pl.BlockSpec((tr, LANES), lambda i: (i, 0))] * 2,
        out_shape=[jax.ShapeDtypeStruct((rows, LANES), F32)] * 2,
    )(pos_rep, inv_rep)
    return cos.reshape(b, s, HALF_ROPE), sin.reshape(b, s, HALF_ROPE)


def _ffn_value(x, g, wg_ref, wu_ref, wd_ref):
    xn = _rms(x, g).astype(BF16)
    acc = None
    for c in range(N_FF_CHUNKS):
        cols = slice(c * FF_CHUNK, (c + 1) * FF_CHUNK)
        gate = _dot(xn, wg_ref[0, :, cols].astype(BF16))
        up = _dot(xn, wu_ref[0, :, cols].astype(BF16))
        a = (gate * jax.nn.sigmoid(gate) * up).astype(BF16)
        d = _dot(a, wd_ref[0, cols, :].astype(BF16))
        acc = d if acc is None else acc + d
    return x + 0.5 * acc


def _ffn_body(x_ref, g_ref, wg_ref, wu_ref, wd_ref, o_ref):
    o_ref[...] = _ffn_value(x_ref[...], g_ref[...], wg_ref, wu_ref, wd_ref)


def _ffn(x2, g, w_gate, w_up, w_down, layer):
    t = x2.shape[0]
    tok = pl.BlockSpec((TM_FFN, D_MODEL), lambda i: (i, 0))
    layer_spec = lambda w: pl.BlockSpec((1,) + w.shape[1:], lambda i: (layer, 0, 0),
                                        pipeline_mode=pl.Buffered(1))
    return pl.pallas_call(
        _ffn_body,
        name="ffn",
        grid=(t // TM_FFN,),
        in_specs=[tok, _const_spec((1, D_MODEL)),
                  layer_spec(w_gate), layer_spec(w_up), layer_spec(w_down)],
        out_specs=tok,
        out_shape=jax.ShapeDtypeStruct((t, D_MODEL), F32),
        compiler_params=pltpu.CompilerParams(
            dimension_semantics=("arbitrary",), vmem_limit_bytes=VMEM_LIMIT),
    )(x2, g, w_gate, w_up, w_down)


def _inproj_body(x_ref, mixg_ref, wa_ref, cqg_ref, ckvg_ref, wq_ref, wk_ref, wvt_ref,
                 ga_ref, gb_ref, gka_ref, gkb_ref, qoff_ref, koff_ref, cos_ref, sin_ref,
                 h_ref, u_ref, q_ref, k_ref, vt_ref):
    h = _rms(x_ref[0], mixg_ref[...]).astype(BF16)
    h_ref[0] = h
    proj = _dot(h, wa_ref[...])
    o1 = 2 * C_CONV
    o2 = o1 + Q_LORA
    o3 = o2 + KV_LORA
    u_ref[0] = proj[:, :C_CONV] * jax.nn.sigmoid(proj[:, C_CONV:o1])
    cqn = _rms(proj[:, o1:o2], cqg_ref[...]).astype(BF16)
    ckvn = _rms(proj[:, o2:o3], ckvg_ref[...]).astype(BF16)
    kra = proj[:, o3:o3 + HEAD_PAD]
    krb = proj[:, o3 + HEAD_PAD:o3 + 2 * HEAD_PAD]
    qq = _dot(cqn, wq_ref[...])
    kv = _dot(ckvn, wk_ref[...])
    vt_ref[0, 0] = lax.dot_general(wvt_ref[...], ckvn, (((1,), (1,)), ((), ())),
                                   preferred_element_type=F32).astype(BF16)
    cos = cos_ref[0]
    sin = sin_ref[0]
    tm = cos.shape[0]
    ca = jnp.concatenate([jnp.ones((tm, NOPE_DIM), F32), cos, cos,
                          jnp.zeros((tm, HEAD_PAD - QK_DIM), F32)], axis=1)
    sb = jnp.concatenate([jnp.zeros((tm, NOPE_DIM), F32), sin, sin,
                          jnp.zeros((tm, HEAD_PAD - QK_DIM), F32)], axis=1)
    q_cos = ga_ref[...] * ca
    q_sin = gb_ref[...] * sb
    gka = gka_ref[...]
    k_rot = kra * (gka * ca) + krb * (gkb_ref[...] * sb)
    ss_rope = jnp.sum(kra * kra, axis=-1, keepdims=True)
    hw = N_HEADS * HEAD_PAD
    for hh in range(N_HEADS):
        lo = hh * HEAD_PAD
        qa = qq[:, lo:lo + HEAD_PAD]
        qb = qq[:, hw + lo:hw + lo + HEAD_PAD]
        rq = lax.rsqrt(jnp.sum(qa * qa, axis=-1, keepdims=True) / QK_DIM + EPS)
        q_ref[0, hh] = (rq * (qa * q_cos + qb * q_sin) + qoff_ref[...]).astype(BF16)
        kn = kv[:, lo:lo + HEAD_PAD]
        ss = jnp.sum(kn * kn, axis=-1, keepdims=True) + ss_rope
        rk = lax.rsqrt(ss / QK_DIM + EPS)
        k_ref[0, hh] = (rk * (kn * gka + k_rot) + koff_ref[...]).astype(BF16)


def _inproj(x3, mixg, wa, cqg, ckvg, wq, wk, wvt, ga, gb, gka, gkb, q_off, k_off,
            cos, sin):
    b, s, _ = x3.shape
    tm = TM_PROJ
    hv = N_HEADS * V_DIM
    tok = lambda w: pl.BlockSpec((1, tm, w), lambda bi, i: (bi, i, 0))
    head = pl.BlockSpec((1, N_HEADS, tm, HEAD_PAD), lambda bi, i: (bi, 0, i, 0))
    vt_spec = pl.BlockSpec((1, 1, hv, tm), lambda bi, i: (bi, i, 0, 0))
    return pl.pallas_call(
        _inproj_body,
        name="inproj",
        grid=(b, s // tm),
        in_specs=[tok(D_MODEL), _const_spec(mixg.shape), _const_spec(wa.shape),
                  _const_spec(cqg.shape), _const_spec(ckvg.shape),
                  _const_spec(wq.shape), _const_spec(wk.shape), _const_spec(wvt.shape),
                  _const_spec(ga.shape), _const_spec(gb.shape),
                  _const_spec(gka.shape), _const_spec(gkb.shape),
                  _const_spec(q_off.shape), _const_spec(k_off.shape),
                  tok(HALF_ROPE), tok(HALF_ROPE)],
        out_specs=[tok(D_MODEL), tok(C_CONV), head, head, vt_spec],
        out_shape=[jax.ShapeDtypeStruct((b, s, D_MODEL), BF16),
                   jax.ShapeDtypeStruct((b, s, C_CONV), F32),
                   jax.ShapeDtypeStruct((b, N_HEADS, s, HEAD_PAD), BF16),
                   jax.ShapeDtypeStruct((b, N_HEADS, s, HEAD_PAD), BF16),
                   jax.ShapeDtypeStruct((b, s // tm, hv, tm), BF16)],
        compiler_params=pltpu.CompilerParams(
            dimension_semantics=("arbitrary", "arbitrary"),
            vmem_limit_bytes=VMEM_LIMIT),
    )(x3, mixg, wa, cqg, ckvg, wq, wk, wvt, ga, gb, gka, gkb, q_off, k_off, cos, sin)


def _pad_lanes(w, lo, total=HEAD_PAD):
    pad = [(0, 0)] * (w.ndim - 1) + [(lo, total - lo - w.shape[-1])]
    return jnp.pad(w, pad)


def _swap_halves(w):
    return jnp.concatenate([w[..., HALF_ROPE:], w[..., :HALF_ROPE]], axis=-1)


def _prep_mixer_weights(w_in, w_uq, w_ukv, q_norm, k_norm):
    o1 = 2 * C_CONV
    o3 = o1 + Q_LORA + KV_LORA
    o4 = o3 + ROPE_DIM
    w_rope = w_in[:, o3:o4]
    wa = jnp.concatenate(
        [w_in[:, :o3], _pad_lanes(w_rope, NOPE_DIM),
         _pad_lanes(_swap_halves(w_rope), NOPE_DIM)], axis=-1).astype(BF16)
    w_gates = w_in[:, o4:].astype(BF16)

    wq3 = w_uq.reshape(Q_LORA, N_HEADS, QK_DIM)
    wq_a = _pad_lanes(wq3, 0).reshape(Q_LORA, N_HEADS * HEAD_PAD)
    wq_b = _pad_lanes(_swap_halves(wq3[..., NOPE_DIM:]), NOPE_DIM)
    wq_b = wq_b.reshape(Q_LORA, N_HEADS * HEAD_PAD)
    wq = jnp.concatenate([wq_a, wq_b], axis=-1).astype(BF16)

    wkv3 = w_ukv.reshape(KV_LORA, N_HEADS, NOPE_DIM + V_DIM)
    wk = _pad_lanes(wkv3[..., :NOPE_DIM], 0).reshape(KV_LORA, N_HEADS * HEAD_PAD)
    wk = wk.astype(BF16)
    wvt = wkv3[..., NOPE_DIM:].reshape(KV_LORA, N_HEADS * V_DIM).T.astype(BF16)

    q_scale = QK_DIM ** -0.5 * math.log2(math.e)
    sign = jnp.concatenate([-jnp.ones((HALF_ROPE,), F32), jnp.ones((HALF_ROPE,), F32)])
    ga = _pad_lanes(q_norm * q_scale, 0).reshape(1, HEAD_PAD)
    gb = _pad_lanes(_swap_halves(q_norm[NOPE_DIM:]) * sign * q_scale, NOPE_DIM)
    gb = gb.reshape(1, HEAD_PAD)
    gka = _pad_lanes(k_norm, 0).reshape(1, HEAD_PAD)
    gkb = _pad_lanes(_swap_halves(k_norm[NOPE_DIM:]) * sign, NOPE_DIM).reshape(1, HEAD_PAD)

    bound = QK_DIM * q_scale * jnp.max(jnp.abs(q_norm)) * jnp.max(jnp.abs(k_norm))
    bounded = bound <= MAX_FIXED_OFFSET
    lane = jnp.arange(HEAD_PAD) == QK_DIM
    q_off = jnp.where(lane & bounded, -bound, 0.0).astype(F32).reshape(1, HEAD_PAD)
    k_off = jnp.where(lane, 1.0, 0.0).astype(F32).reshape(1, HEAD_PAD)
    return wa, w_gates, wq, wk, wvt, ga, gb, gka, gkb, q_off, k_off, bounded


def _attn_body(q_ref, k_ref, vt_ref, o_ref, m_ref, l_ref, acc_ref):
    i = pl.program_id(2)
    key_chunk = lax.broadcasted_iota(jnp.int32, (TK, TQ), 0) // CHUNK
    qry_chunk = lax.broadcasted_iota(jnp.int32, (TK, TQ), 1) // CHUNK
    diag_mask = key_chunk <= qry_chunk

    def scores_t(hh, j):
        kb = k_ref[0, hh, pl.ds(pl.multiple_of(j * TK, TK), TK), :]
        return lax.dot_general(kb, q_ref[0, hh], (((1,), (1,)), ((), ())),
                               preferred_element_type=F32)

    def values_t(hh, j):
        return vt_ref[0, j, hh * V_DIM:(hh + 1) * V_DIM, :]

    for hh in range(2):
        s = jnp.where(diag_mask, scores_t(hh, i), -jnp.inf)
        m = jnp.max(s, axis=0, keepdims=True)
        p = jnp.exp2(s - m)
        m_ref[hh] = m
        l_ref[hh] = jnp.sum(p, axis=0, keepdims=True)
        acc_ref[hh] = _dot(values_t(hh, i), p.astype(BF16))

    def step(j, carry):
        for hh in range(2):
            s = scores_t(hh, j)
            m_old = m_ref[hh]
            m_new = jnp.maximum(m_old, jnp.max(s, axis=0, keepdims=True))
            alpha = jnp.exp2(m_old - m_new)
            p = jnp.exp2(s - m_new)
            l_ref[hh] = alpha * l_ref[hh] + jnp.sum(p, axis=0, keepdims=True)
            acc_ref[hh] = alpha * acc_ref[hh] + _dot(values_t(hh, j), p.astype(BF16))
            m_ref[hh] = m_new
        return carry

    lax.fori_loop(0, i, step, 0)

    o_t = jnp.concatenate([acc_ref[0] / l_ref[0], acc_ref[1] / l_ref[1]], axis=0)
    o_ref[0] = o_t.T.astype(BF16)


def _attn_bounded_body(q_ref, k_ref, vt_ref, o_ref):
    qb = ATTN_Q_BLOCK
    n_tiles = q_ref.shape[2] // qb

    n_sub = qb // TK

    def diag_mask(nq):
        key_chunk = lax.broadcasted_iota(jnp.int32, (TK, nq), 0) // CHUNK
        qry_chunk = lax.broadcasted_iota(jnp.int32, (TK, nq), 1) // CHUNK
        return key_chunk <= qry_chunk

    masks = {qb - a * TK: diag_mask(qb - a * TK) for a in range(n_sub)}

    for i in range(n_tiles):
        q_lo = i * qb
        sums = [[None] * n_sub for _ in range(2)]
        accs = [[None] * n_sub for _ in range(2)]
        for hh in range(2):
            for j in range((q_lo + qb) // TK):
                first = max(0, j - q_lo // TK)
                nq = qb - first * TK
                q = q_ref[0, hh, q_lo + first * TK:q_lo + qb, :]
                kb = k_ref[0, hh, j * TK:(j + 1) * TK, :]
                p = jnp.exp2(lax.dot_general(kb, q, (((1,), (1,)), ((), ())),
                                             preferred_element_type=F32))
                if j * TK >= q_lo:
                    p = jnp.where(masks[nq], p, 0.0)
                ps = jnp.sum(p.reshape(TK // SUBLANES, SUBLANES, nq), axis=0)
                pv = _dot(vt_ref[0, j, hh * V_DIM:(hh + 1) * V_DIM, :], p.astype(BF16))
                for a in range(first, n_sub):
                    cols = slice((a - first) * TK, (a - first + 1) * TK)
                    sums[hh][a] = ps[:, cols] if sums[hh][a] is None else sums[hh][a] + ps[:, cols]
                    accs[hh][a] = pv[:, cols] if accs[hh][a] is None else accs[hh][a] + pv[:, cols]
        for a in range(n_sub):
            o_t = jnp.concatenate(
                [accs[hh][a] / jnp.sum(sums[hh][a], axis=0, keepdims=True) for hh in range(2)],
                axis=0)
            o_ref[0, q_lo + a * TK:q_lo + (a + 1) * TK, :] = o_t.T.astype(BF16)


def _attention_bounded(q, k, vt):
    b, _, s, _ = q.shape
    return pl.pallas_call(
        _attn_bounded_body,
        name="attn_bounded",
        grid=(b, N_HEADS // 2),
        in_specs=[pl.BlockSpec((1, 2, s, HEAD_PAD), lambda bi, p: (bi, p, 0, 0)),
                  pl.BlockSpec((1, 2, s, HEAD_PAD), lambda bi, p: (bi, p, 0, 0)),
                  pl.BlockSpec((1, s // TK, 2 * V_DIM, TK), lambda bi, p: (bi, 0, p, 0))],
        out_specs=pl.BlockSpec((1, s, 2 * V_DIM), lambda bi, p: (bi, 0, p)),
        out_shape=jax.ShapeDtypeStruct((b, s, N_HEADS * V_DIM), BF16),
        compiler_params=pltpu.CompilerParams(
            dimension_semantics=("arbitrary", "arbitrary"),
            vmem_limit_bytes=VMEM_LIMIT),
    )(q, k, vt)


def _attention(q, k, vt):
    b, _, s, _ = q.shape
    assert TQ == TK
    scratch = [pltpu.VMEM((2, 1, TQ), F32), pltpu.VMEM((2, 1, TQ), F32),
               pltpu.VMEM((2, V_DIM, TQ), F32)]
    return pl.pallas_call(
        _attn_body,
        name="attn",
        grid=(b, N_HEADS // 2, s // TQ),
        in_specs=[pl.BlockSpec((1, 2, TQ, HEAD_PAD), lambda bi, p, i: (bi, p, i, 0)),
                  pl.BlockSpec((1, 2, s, HEAD_PAD), lambda bi, p, i: (bi, p, 0, 0)),
                  pl.BlockSpec((1, s // TK, 2 * V_DIM, TK), lambda bi, p, i: (bi, 0, p, 0))],
        out_specs=pl.BlockSpec((1, TQ, 2 * V_DIM), lambda bi, p, i: (bi, i, p)),
        out_shape=jax.ShapeDtypeStruct((b, s, N_HEADS * V_DIM), BF16),
        scratch_shapes=scratch,
        compiler_params=pltpu.CompilerParams(
            dimension_semantics=("arbitrary", "arbitrary", "arbitrary"),
            vmem_limit_bytes=VMEM_LIMIT),
    )(q, k, vt)


def _conv_fill(u, halo, ubuf_ref, shift_ref):
    ubuf_ref[0:HALO, :] = halo
    ubuf_ref[HALO:HALO + TM_MIX, :] = u
    for bb in range(1, SUBLANES):
        shift_ref[bb - 1] = ubuf_ref[bb:bb + SHIFT_ROWS, :]


def _conv_branch(row0, nrows, cw_ref, cb_ref, lng_ref, lnb_ref, ubuf_ref, shift_ref, conv_ref):
    first = HALO - (CONV_WIDTH - 1)
    for cb in range(C_CONV // LANES):
        cols = slice(cb * LANES, (cb + 1) * LANES)
        for rb in range(row0 // CONV_ROWS, (row0 + nrows) // CONV_ROWS):
            acc = None
            for kk in range(CONV_WIDTH):
                aa, bb = divmod(first + kk, SUBLANES)
                r0 = aa * SUBLANES + rb * CONV_ROWS
                if bb == 0:
                    win = ubuf_ref[r0:r0 + CONV_ROWS, cols]
                else:
                    win = shift_ref[bb - 1, r0:r0 + CONV_ROWS, cols]
                term = cw_ref[kk:kk + 1, cols] * win
                acc = term if acc is None else acc + term
            conv_ref[rb * CONV_ROWS:(rb + 1) * CONV_ROWS, cols] = acc + cb_ref[:, cols]
    conv = conv_ref[row0:row0 + nrows, :]
    mu = jnp.mean(conv, axis=-1, keepdims=True)
    cc = conv - mu
    ln = cc * lax.rsqrt(jnp.mean(cc * cc, axis=-1, keepdims=True) + EPS)
    ln = ln * lng_ref[...] + lnb_ref[...]
    return (ln * jax.nn.sigmoid(ln)).astype(BF16)


def _mix_body(x_ref, h_ref, u_ref, halo_ref, o_ref, wg_ref, gbias_ref,
              cw_ref, cb_ref, lng_ref, lnb_ref, wco_ref, wmo_ref, wout_ref,
              out_ref, ubuf_ref, shift_ref, conv_ref):
    i = pl.program_id(1)
    halo = halo_ref[0]
    halo = jnp.where(i > 0, halo, jnp.zeros_like(halo))
    _conv_fill(u_ref[0], halo, ubuf_ref, shift_ref)
    act = _conv_branch(0, TM_MIX, cw_ref, cb_ref, lng_ref, lnb_ref,
                       ubuf_ref, shift_ref, conv_ref)
    gates = jax.nn.sigmoid(_dot(h_ref[0], wg_ref[...]) + gbias_ref[...])
    y_conv = _dot(act, wco_ref[...])
    y_mla = _dot(o_ref[0], wmo_ref[...])
    y = gates[:, :D_MODEL] * y_conv + gates[:, D_MODEL:] * y_mla
    out_ref[0] = x_ref[0] + _dot(y.astype(BF16), wout_ref[...])


def _mix(x3, h, u, o, w_gates, gbias, cw, cb, lng, lnb, wco, wmo, wout):
    b, s, _ = x3.shape
    tm = TM_MIX
    tok = lambda w: pl.BlockSpec((1, tm, w), lambda bi, i: (bi, i, 0))
    halo = pl.BlockSpec(
        (1, HALO, C_CONV), lambda bi, i: (bi, jnp.maximum(i * (tm // HALO) - 1, 0), 0))
    consts = [w_gates, gbias, cw, cb, lng, lnb, wco, wmo, wout]
    return pl.pallas_call(
        _mix_body,
        name="mix",
        grid=(b, s // tm),
        in_specs=[tok(D_MODEL), tok(D_MODEL), tok(C_CONV), halo, tok(N_HEADS * V_DIM)]
                 + [_const_spec(c.shape) for c in consts],
        out_specs=tok(D_MODEL),
        out_shape=jax.ShapeDtypeStruct((b, s, D_MODEL), F32),
        scratch_shapes=[pltpu.VMEM((HALO + tm, C_CONV), F32),
                        pltpu.VMEM((SUBLANES - 1, SHIFT_ROWS, C_CONV), F32),
                        pltpu.VMEM((tm, C_CONV), F32)],
        compiler_params=pltpu.CompilerParams(
            dimension_semantics=("arbitrary", "arbitrary"),
            vmem_limit_bytes=VMEM_LIMIT),
    )(x3, h, u, u, o, *consts)


def kernel(x, positions, ffn1_norm, ffn1_w_gate, ffn1_w_up, ffn1_w_down, mix_norm, w_in,
           gate_bias, conv_w, conv_b, conv_ln_g, conv_ln_b, w_conv_out, cq_norm, ckv_norm,
           w_uq, w_ukv, q_norm, k_norm, w_mla_out, w_out, ffn2_norm, ffn2_w_gate, ffn2_w_up,
           ffn2_w_down):
    b, s, d = x.shape
    depth = ffn1_norm.shape[0]
    assert d == D_MODEL and s % max(TM_PROJ, TM_MIX, TQ, ATTN_Q_BLOCK) == 0
    assert (b * s) % TM_FFN == 0 and ATTN_Q_BLOCK % TK == 0
    row = lambda v: v.reshape(1, -1)
    cos, sin = _rope_tables(positions)
    ffn1_w = (ffn1_w_gate, ffn1_w_up, ffn1_w_down)
    ffn2_w = (ffn2_w_gate, ffn2_w_up, ffn2_w_down)
    for l in range(depth):
        x = _ffn(x.reshape(b * s, d), row(ffn1_norm[l]), *ffn1_w, l).reshape(b, s, d)

        wa, w_gates, wq, wk, wvt, ga, gb, gka, gkb, q_off, k_off, bounded = _prep_mixer_weights(
            w_in[l], w_uq[l], w_ukv[l], q_norm[l], k_norm[l])
        h, u, q, k, vt = _inproj(x, row(mix_norm[l]), wa, row(cq_norm[l]), row(ckv_norm[l]),
                                 wq, wk, wvt, ga, gb, gka, gkb, q_off, k_off, cos, sin)
        o = lax.cond(bounded, _attention_bounded, _attention, q, k, vt)
        x = _mix(x, h, u, o, w_gates, gate_bias[l].reshape(1, 2 * D_MODEL),
                 conv_w[l], row(conv_b[l]), row(conv_ln_g[l]), row(conv_ln_b[l]),
                 w_conv_out[l].astype(BF16), w_mla_out[l].astype(BF16),
                 w_out[l].astype(BF16))

        x = _ffn(x.reshape(b * s, d), row(ffn2_norm[l]), *ffn2_w, l).reshape(b, s, d)
    return x
```

```python
import math

import jax
import jax.numpy as jnp
from jax import lax
from jax.experimental import pallas as pl
from jax.experimental.pallas import tpu as pltpu

D_MODEL = 1024
D_FF = 2816
C_CONV = 512
CONV_WIDTH = 31
N_HEADS = 8
NOPE_DIM = 64
ROPE_DIM = 32
QK_DIM = NOPE_DIM + ROPE_DIM
V_DIM = 64
Q_LORA = 256
KV_LORA = 256
CHUNK = 64
ROPE_THETA = 10000.0
EPS = 1e-6

LANES = 128
SUBLANES = 8
MAX_FIXED_OFFSET = 50.0
HEAD_PAD = LANES
HALF_ROPE = ROPE_DIM // 2
VMEM_LIMIT = 56 * 1024 * 1024
FF_CHUNK = 256
N_FF_CHUNKS = D_FF // FF_CHUNK
TM_FFN = 1024
TM_PROJ = 1024
TM_MIX = 1024
TQ = 512
TK = 512
ATTN_Q_BLOCK = 2048
HALO = 32
SHIFT_ROWS = TM_MIX + HALO - SUBLANES
CONV_ROWS = 64
F32 = jnp.float32
BF16 = jnp.bfloat16


def _const_spec(shape):
    zeros = (0,) * len(shape)
    return pl.BlockSpec(shape, lambda *_: zeros, pipeline_mode=pl.Buffered(1))


def _rms(x, g):
    ms = jnp.mean(x * x, axis=-1, keepdims=True)
    return x * lax.rsqrt(ms + EPS) * g


def _dot(a, b):
    return jnp.dot(a, b, preferred_element_type=F32)


def _rope_body(pos_ref, inv_ref, cos_ref, sin_ref):
    ang = pos_ref[...].astype(F32) * inv_ref[...]
    cos_ref[...] = jnp.cos(ang)
    sin_ref[...] = jnp.sin(ang)


def _rope_tables(positions):
    b, s = positions.shape
    t = b * s
    rows = t * HALF_ROPE // LANES
    inv_freq = ROPE_THETA ** (-jnp.arange(0, ROPE_DIM, 2, dtype=F32) / ROPE_DIM)
    pos_rep = jnp.broadcast_to(positions.reshape(t, 1), (t, HALF_ROPE)).reshape(rows, LANES)
    inv_rep = jnp.tile(inv_freq, LANES // HALF_ROPE).reshape(1, LANES)
    tr = min(rows, TM_PROJ)
    cos, sin = pl.pallas_call(
        _rope_body,
        name="rope_tables",
        grid=(rows // tr,),
        in_specs=[pl.BlockSpec((tr, LANES), lambda i: (i, 0)),
                  pl.BlockSpec((1, LANES), lambda i: (0, 0))],
        out_specs=[pl.BlockSpec((tr, LANES), lambda i: (i, 0))] * 2,
        out_shape=[jax.ShapeDtypeStruct((rows, LANES), F32)] * 2,
    )(pos_rep, inv_rep)
    return cos.reshape(b, s, HALF_ROPE), sin.reshape(b, s, HALF_ROPE)


def _ffn_value(x, g, wg_ref, wu_ref, wd_ref):
    xn = _rms(x, g).astype(BF16)
    acc = None
    for c in range(N_FF_CHUNKS):
        cols = slice(c * FF_CHUNK, (c + 1) * FF_CHUNK)
        gate = _dot(xn, wg_ref[0, :, cols].astype(BF16))
        up = _dot(xn, wu_ref[0, :, cols].astype(BF16))
        a = (gate * jax.nn.sigmoid(gate) * up).astype(BF16)
        d = _dot(a, wd_ref[0, cols, :].astype(BF16))
        acc = d if acc is None else acc + d
    return x + 0.5 * acc


def _ffn_body(x_ref, g_ref, wg_ref, wu_ref, wd_ref, o_ref):
    o_ref[...] = _ffn_value(x_ref[...], g_ref[...], wg_ref, wu_ref, wd_ref)


def _ffn(x2, g, w_gate, w_up, w_down, layer):
    t = x2.shape[0]
    tok = pl.BlockSpec((TM_FFN, D_MODEL), lambda i: (i, 0))
    layer_spec = lambda w: pl.BlockSpec((1,) + w.shape[1:], lambda i: (layer, 0, 0),
                                        pipeline_mode=pl.Buffered(1))
    return pl.pallas_call(
        _ffn_body,
        name="ffn",
        grid=(t // TM_FFN,),
        in_specs=[tok, _const_spec((1, D_MODEL)),
                  layer_spec(w_gate), layer_spec(w_up), layer_spec(w_down)],
        out_specs=tok,
        out_shape=jax.ShapeDtypeStruct((t, D_MODEL), F32),
        compiler_params=pltpu.CompilerParams(
            dimension_semantics=("arbitrary",), vmem_limit_bytes=VMEM_LIMIT),
    )(x2, g, w_gate, w_up, w_down)


def _inproj_body(x_ref, mixg_ref, wa_ref, cqg_ref, ckvg_ref, wq_ref, wk_ref, wvt_ref,
                 ga_ref, gb_ref, gka_ref, gkb_ref, qoff_ref, koff_ref, cos_ref, sin_ref,
                 h_ref, u_ref, q_ref, k_ref, vt_ref):
    h = _rms(x_ref[0], mixg_ref[...]).astype(BF16)
    h_ref[0] = h
    proj = _dot(h, wa_ref[...])
    o1 = 2 * C_CONV
    o2 = o1 + Q_LORA
    o3 = o2 + KV_LORA
    u_ref[0] = proj[:, :C_CONV] * jax.nn.sigmoid(proj[:, C_CONV:o1])
    cqn = _rms(proj[:, o1:o2], cqg_ref[...]).astype(BF16)
    ckvn = _rms(proj[:, o2:o3], ckvg_ref[...]).astype(BF16)
    kra = proj[:, o3:o3 + HEAD_PAD]
    krb = proj[:, o3 + HEAD_PAD:o3 + 2 * HEAD_PAD]
    qq = _dot(cqn, wq_ref[...])
    kv = _dot(ckvn, wk_ref[...])
    v_t = lax.dot_general(wvt_ref[...], ckvn, (((1,), (1,)), ((), ())),
                          preferred_element_type=F32).astype(BF16)
    for t in range(v_t.shape[1] // TK):
        vt_ref[0, t] = v_t[:, t * TK:(t + 1) * TK]
    cos = cos_ref[0]
    sin = sin_ref[0]
    tm = cos.shape[0]
    ca = jnp.concatenate([jnp.ones((tm, NOPE_DIM), F32), cos, cos,
                          jnp.zeros((tm, HEAD_PAD - QK_DIM), F32)], axis=1)
    sb = jnp.concatenate([jnp.zeros((tm, NOPE_DIM), F32), sin, sin,
                          jnp.zeros((tm, HEAD_PAD - QK_DIM), F32)], axis=1)
    q_cos = ga_ref[...] * ca
    q_sin = gb_ref[...] * sb
    gka = gka_ref[...]
    k_rot = kra * (gka * ca) + krb * (gkb_ref[...] * sb)
    ss_rope = jnp.sum(kra * kra, axis=-1, keepdims=True)
    hw = N_HEADS * HEAD_PAD
    for hh in range(N_HEADS):
        lo = hh * HEAD_PAD
        qa = qq[:, lo:lo + HEAD_PAD]
        qb = qq[:, hw + lo:hw + lo + HEAD_PAD]
        rq = lax.rsqrt(jnp.sum(qa * qa, axis=-1, keepdims=True) / QK_DIM + EPS)
        q_ref[0, hh] = (rq * (qa * q_cos + qb * q_sin) + qoff_ref[...]).astype(BF16)
        kn = kv[:, lo:lo + HEAD_PAD]
        ss = jnp.sum(kn * kn, axis=-1, keepdims=True) + ss_rope
        rk = lax.rsqrt(ss / QK_DIM + EPS)
        k_ref[0, hh] = (rk * (kn * gka + k_rot) + koff_ref[...]).astype(BF16)


def _inproj(x3, mixg, wa, cqg, ckvg, wq, wk, wvt, ga, gb, gka, gkb, q_off, k_off,
            cos, sin):
    b, s, _ = x3.shape
    tm = TM_PROJ
    hv = N_HEADS * V_DIM
    tok = lambda w: pl.BlockSpec((1, tm, w), lambda bi, i: (bi, i, 0))
    head = pl.BlockSpec((1, N_HEADS, tm, HEAD_PAD), lambda bi, i: (bi, 0, i, 0))
    vt_spec = pl.BlockSpec((1, tm // TK, hv, TK), lambda bi, i: (bi, i, 0, 0))
    return pl.pallas_call(
        _inproj_body,
        name="inproj",
        grid=(b, s // tm),
        in_specs=[tok(D_MODEL), _const_spec(mixg.shape), _const_spec(wa.shape),
                  _const_spec(cqg.shape), _const_spec(ckvg.shape),
                  _const_spec(wq.shape), _const_spec(wk.shape), _const_spec(wvt.shape),
                  _const_spec(ga.shape), _const_spec(gb.shape),
                  _const_spec(gka.shape), _const_spec(gkb.shape),
                  _const_spec(q_off.shape), _const_spec(k_off.shape),
                  tok(HALF_ROPE), tok(HALF_ROPE)],
        out_specs=[tok(D_MODEL), tok(C_CONV), head, head, vt_spec],
        out_shape=[jax.ShapeDtypeStruct((b, s, D_MODEL), BF16),
                   jax.ShapeDtypeStruct((b, s, C_CONV), F32),
                   jax.ShapeDtypeStruct((b, N_HEADS, s, HEAD_PAD), BF16),
                   jax.ShapeDtypeStruct((b, N_HEADS, s, HEAD_PAD), BF16),
                   jax.ShapeDtypeStruct((b, s // TK, hv, TK), BF16)],
        compiler_params=pltpu.CompilerParams(
            dimension_semantics=("arbitrary", "arbitrary"),
            vmem_limit_bytes=VMEM_LIMIT),
    )(x3, mixg, wa, cqg, ckvg, wq, wk, wvt, ga, gb, gka, gkb, q_off, k_off, cos, sin)


def _pad_lanes(w, lo, total=HEAD_PAD):
    pad = [(0, 0)] * (w.ndim - 1) + [(lo, total - lo - w.shape[-1])]
    return jnp.pad(w, pad)


def _swap_halves(w):
    return jnp.concatenate([w[..., HALF_ROPE:], w[..., :HALF_ROPE]], axis=-1)


def _prep_mixer_weights(w_in, w_uq, w_ukv, q_norm, k_norm):
    o1 = 2 * C_CONV
    o3 = o1 + Q_LORA + KV_LORA
    o4 = o3 + ROPE_DIM
    w_rope = w_in[:, o3:o4]
    wa = jnp.concatenate(
        [w_in[:, :o3], _pad_lanes(w_rope, NOPE_DIM),
         _pad_lanes(_swap_halves(w_rope), NOPE_DIM)], axis=-1).astype(BF16)
    w_gates = w_in[:, o4:].astype(BF16)

    wq3 = w_uq.reshape(Q_LORA, N_HEADS, QK_DIM)
    wq_a = _pad_lanes(wq3, 0).reshape(Q_LORA, N_HEADS * HEAD_PAD)
    wq_b = _pad_lanes(_swap_halves(wq3[..., NOPE_DIM:]), NOPE_DIM)
    wq_b = wq_b.reshape(Q_LORA, N_HEADS * HEAD_PAD)
    wq = jnp.concatenate([wq_a, wq_b], axis=-1).astype(BF16)

    wkv3 = w_ukv.reshape(KV_LORA, N_HEADS, NOPE_DIM + V_DIM)
    wk = _pad_lanes(wkv3[..., :NOPE_DIM], 0).reshape(KV_LORA, N_HEADS * HEAD_PAD)
    wk = wk.astype(BF16)
    wvt = wkv3[..., NOPE_DIM:].reshape(KV_LORA, N_HEADS * V_DIM).T.astype(BF16)

    q_scale = QK_DIM ** -0.5 * math.log2(math.e)
    sign = jnp.concatenate([-jnp.ones((HALF_ROPE,), F32), jnp.ones((HALF_ROPE,), F32)])
    ga = _pad_lanes(q_norm * q_scale, 0).reshape(1, HEAD_PAD)
    gb = _pad_lanes(_swap_halves(q_norm[NOPE_DIM:]) * sign * q_scale, NOPE_DIM)
    gb = gb.reshape(1, HEAD_PAD)
    gka = _pad_lanes(k_norm, 0).reshape(1, HEAD_PAD)
    gkb = _pad_lanes(_swap_halves(k_norm[NOPE_DIM:]) * sign, NOPE_DIM).reshape(1, HEAD_PAD)

    bound = QK_DIM * q_scale * jnp.max(jnp.abs(q_norm)) * jnp.max(jnp.abs(k_norm))
    bounded = bound <= MAX_FIXED_OFFSET
    lane = jnp.arange(HEAD_PAD) == QK_DIM
    q_off = jnp.where(lane & bounded, -bound, 0.0).astype(F32).reshape(1, HEAD_PAD)
    k_off = jnp.where(lane, 1.0, 0.0).astype(F32).reshape(1, HEAD_PAD)
    return wa, w_gates, wq, wk, wvt, ga, gb, gka, gkb, q_off, k_off, bounded


def _attn_body(q_ref, k_ref, vt_ref, o_ref, m_ref, l_ref, acc_ref):
    i = pl.program_id(2)
    key_chunk = lax.broadcasted_iota(jnp.int32, (TK, TQ), 0) // CHUNK
    qry_chunk = lax.broadcasted_iota(jnp.int32, (TK, TQ), 1) // CHUNK
    diag_mask = key_chunk <= qry_chunk

    def scores_t(hh, j):
        kb = k_ref[0, hh, pl.ds(pl.multiple_of(j * TK, TK), TK), :]
        return lax.dot_general(kb, q_ref[0, hh], (((1,), (1,)), ((), ())),
                               preferred_element_type=F32)

    def values_t(hh, j):
        return vt_ref[0, j, hh * V_DIM:(hh + 1) * V_DIM, :]

    for hh in range(2):
        s = jnp.where(diag_mask, scores_t(hh, i), -jnp.inf)
        m = jnp.max(s, axis=0, keepdims=True)
        p = jnp.exp2(s - m)
        m_ref[hh] = m
        l_ref[hh] = jnp.sum(p, axis=0, keepdims=True)
        acc_ref[hh] = _dot(values_t(hh, i), p.astype(BF16))

    def step(j, carry):
        for hh in range(2):
            s = scores_t(hh, j)
            m_old = m_ref[hh]
            m_new = jnp.maximum(m_old, jnp.max(s, axis=0, keepdims=True))
            alpha = jnp.exp2(m_old - m_new)
            p = jnp.exp2(s - m_new)
            l_ref[hh] = alpha * l_ref[hh] + jnp.sum(p, axis=0, keepdims=True)
            acc_ref[hh] = alpha * acc_ref[hh] + _dot(values_t(hh, j), p.astype(BF16))
            m_ref[hh] = m_new
        return carry

    lax.fori_loop(0, i, step, 0)

    o_t = jnp.concatenate([acc_ref[0] / l_ref[0], acc_ref[1] / l_ref[1]], axis=0)
    o_ref[0] = o_t.T.astype(BF16)


def _attn_bounded_body(q_ref, k_ref, vt_ref, o_ref):
    qb = ATTN_Q_BLOCK
    n_tiles = q_ref.shape[2] // qb

    n_sub = qb // TK

    def diag_mask(nq):
        key_chunk = lax.broadcasted_iota(jnp.int32, (TK, nq), 0) // CHUNK
        qry_chunk = lax.broadcasted_iota(jnp.int32, (TK, nq), 1) // CHUNK
        return key_chunk <= qry_chunk

    masks = {qb - a * TK: diag_mask(qb - a * TK) for a in range(n_sub)}

    for i in range(n_tiles):
        q_lo = i * qb
        sums = [[None] * n_sub for _ in range(2)]
        accs = [[None] * n_sub for _ in range(2)]
        for hh in range(2):
            for j in range((q_lo + qb) // TK):
                first = max(0, j - q_lo // TK)
                nq = qb - first * TK
                q = q_ref[0, hh, q_lo + first * TK:q_lo + qb, :]
                kb = k_ref[0, hh, j * TK:(j + 1) * TK, :]
                p = jnp.exp2(lax.dot_general(kb, q, (((1,), (1,)), ((), ())),
                                             preferred_element_type=F32))
                if j * TK >= q_lo:
                    p = jnp.where(masks[nq], p, 0.0)
                ps = jnp.sum(p.reshape(TK // SUBLANES, SUBLANES, nq), axis=0)
                pv = _dot(vt_ref[0, j, hh * V_DIM:(hh + 1) * V_DIM, :], p.astype(BF16))
                for a in range(first, n_sub):
                    cols = slice((a - first) * TK, (a - first + 1) * TK)
                    sums[hh][a] = ps[:, cols] if sums[hh][a] is None else sums[hh][a] + ps[:, cols]
                    accs[hh][a] = pv[:, cols] if accs[hh][a] is None else accs[hh][a] + pv[:, cols]
        for a in range(n_sub):
            o_t = jnp.concatenate(
                [accs[hh][a] / jnp.sum(sums[hh][a], axis=0, keepdims=True) for hh in range(2)],
                axis=0)
            o_ref[0, q_lo + a * TK:q_lo + (a + 1) * TK, :] = o_t.T.astype(BF16)


def _attention_bounded(q, k, vt):
    b, _, s, _ = q.shape
    return pl.pallas_call(
        _attn_bounded_body,
        name="attn_bounded",
        grid=(b, N_HEADS // 2),
        in_specs=[pl.BlockSpec((1, 2, s, HEAD_PAD), lambda bi, p: (bi, p, 0, 0)),
                  pl.BlockSpec((1, 2, s, HEAD_PAD), lambda bi, p: (bi, p, 0, 0)),
                  pl.BlockSpec((1, s // TK, 2 * V_DIM, TK), lambda bi, p: (bi, 0, p, 0))],
        out_specs=pl.BlockSpec((1, s, 2 * V_DIM), lambda bi, p: (bi, 0, p)),
        out_shape=jax.ShapeDtypeStruct((b, s, N_HEADS * V_DIM), BF16),
        compiler_params=pltpu.CompilerParams(
            dimension_semantics=("arbitrary", "arbitrary"),
            vmem_limit_bytes=VMEM_LIMIT),
    )(q, k, vt)


def _attention(q, k, vt):
    b, _, s, _ = q.shape
    assert TQ == TK
    scratch = [pltpu.VMEM((2, 1, TQ), F32), pltpu.VMEM((2, 1, TQ), F32),
               pltpu.VMEM((2, V_DIM, TQ), F32)]
    return pl.pallas_call(
        _attn_body,
        name="attn",
        grid=(b, N_HEADS // 2, s // TQ),
        in_specs=[pl.BlockSpec((1, 2, TQ, HEAD_PAD), lambda bi, p, i: (bi, p, i, 0)),
                  pl.BlockSpec((1, 2, s, HEAD_PAD), lambda bi, p, i: (bi, p, 0, 0)),
                  pl.BlockSpec((1, s // TK, 2 * V_DIM, TK), lambda bi, p, i: (bi, 0, p, 0))],
        out_specs=pl.BlockSpec((1, TQ, 2 * V_DIM), lambda bi, p, i: (bi, i, p)),
        out_shape=jax.ShapeDtypeStruct((b, s, N_HEADS * V_DIM), BF16),
        scratch_shapes=scratch,
        compiler_params=pltpu.CompilerParams(
            dimension_semantics=("arbitrary", "arbitrary", "arbitrary"),
            vmem_limit_bytes=VMEM_LIMIT),
    )(q, k, vt)


def _conv_fill(u, halo, ubuf_ref, shift_ref):
    ubuf_ref[0:HALO, :] = halo
    ubuf_ref[HALO:HALO + TM_MIX, :] = u
    for bb in range(1, SUBLANES):
        shift_ref[bb - 1] = ubuf_ref[bb:bb + SHIFT_ROWS, :]


def _conv_branch(row0, nrows, cw_ref, cb_ref, lng_ref, lnb_ref, ubuf_ref, shift_ref, conv_ref):
    first = HALO - (CONV_WIDTH - 1)
    for cb in range(C_CONV // LANES):
        cols = slice(cb * LANES, (cb + 1) * LANES)
        for rb in range(row0 // CONV_ROWS, (row0 + nrows) // CONV_ROWS):
            acc = None
            for kk in range(CONV_WIDTH):
                aa, bb = divmod(first + kk, SUBLANES)
                r0 = aa * SUBLANES + rb * CONV_ROWS
                if bb == 0:
                    win = ubuf_ref[r0:r0 + CONV_ROWS, cols]
                else:
                    win = shift_ref[bb - 1, r0:r0 + CONV_ROWS, cols]
                term = cw_ref[kk:kk + 1, cols] * win
                acc = term if acc is None else acc + term
            conv_ref[rb * CONV_ROWS:(rb + 1) * CONV_ROWS, cols] = acc + cb_ref[:, cols]
    conv = conv_ref[row0:row0 + nrows, :]
    mu = jnp.mean(conv, axis=-1, keepdims=True)
    cc = conv - mu
    ln = cc * lax.rsqrt(jnp.mean(cc * cc, axis=-1, keepdims=True) + EPS)
    ln = ln * lng_ref[...] + lnb_ref[...]
    return (ln * jax.nn.sigmoid(ln)).astype(BF16)


def _mix_body(x_ref, h_ref, u_ref, halo_ref, o_ref, wg_ref, gbias_ref,
              cw_ref, cb_ref, lng_ref, lnb_ref, wco_ref, wmo_ref, wout_ref,
              out_ref, ubuf_ref, shift_ref, conv_ref):
    i = pl.program_id(1)
    halo = halo_ref[0]
    halo = jnp.where(i > 0, halo, jnp.zeros_like(halo))
    _conv_fill(u_ref[0], halo, ubuf_ref, shift_ref)
    act = _conv_branch(0, TM_MIX, cw_ref, cb_ref, lng_ref, lnb_ref,
                       ubuf_ref, shift_ref, conv_ref)
    gates = jax.nn.sigmoid(_dot(h_ref[0], wg_ref[...]) + gbias_ref[...])
    y_conv = _dot(act, wco_ref[...])
    y_mla = _dot(o_ref[0], wmo_ref[...])
    y = gates[:, :D_MODEL] * y_conv + gates[:, D_MODEL:] * y_mla
    out_ref[0] = x_ref[0] + _dot(y.astype(BF16), wout_ref[...])


def _mix(x3, h, u, o, w_gates, gbias, cw, cb, lng, lnb, wco, wmo, wout):
    b, s, _ = x3.shape
    tm = TM_MIX
    tok = lambda w: pl.BlockSpec((1, tm, w), lambda bi, i: (bi, i, 0))
    halo = pl.BlockSpec(
        (1, HALO, C_CONV), lambda bi, i: (bi, jnp.maximum(i * (tm // HALO) - 1, 0), 0))
    consts = [w_gates, gbias, cw, cb, lng, lnb, wco, wmo, wout]
    return pl.pallas_call(
        _mix_body,
        name="mix",
        grid=(b, s // tm),
        in_specs=[tok(D_MODEL), tok(D_MODEL), tok(C_CONV), halo, tok(N_HEADS * V_DIM)]
                 + [_const_spec(c.shape) for c in consts],
        out_specs=tok(D_MODEL),
        out_shape=jax.ShapeDtypeStruct((b, s, D_MODEL), F32),
        scratch_shapes=[pltpu.VMEM((HALO + tm, C_CONV), F32),
                        pltpu.VMEM((SUBLANES - 1, SHIFT_ROWS, C_CONV), F32),
                        pltpu.VMEM((tm, C_CONV), F32)],
        compiler_params=pltpu.CompilerParams(
            dimension_semantics=("arbitrary", "arbitrary"),
            vmem_limit_bytes=VMEM_LIMIT),
    )(x3, h, u, u, o, *consts)


def kernel(x, positions, ffn1_norm, ffn1_w_gate, ffn1_w_up, ffn1_w_down, mix_norm, w_in,
           gate_bias, conv_w, conv_b, conv_ln_g, conv_ln_b, w_conv_out, cq_norm, ckv_norm,
           w_uq, w_ukv, q_norm, k_norm, w_mla_out, w_out, ffn2_norm, ffn2_w_gate, ffn2_w_up,
           ffn2_w_down):
    b, s, d = x.shape
    depth = ffn1_norm.shape[0]
    assert d == D_MODEL and s % max(TM_PROJ, TM_MIX, TQ, ATTN_Q_BLOCK) == 0
    assert (b * s) % TM_FFN == 0 and ATTN_Q_BLOCK % TK == 0 and TM_PROJ % TK == 0
    row = lambda v: v.reshape(1, -1)
    cos, sin = _rope_tables(positions)
    ffn1_w = (ffn1_w_gate, ffn1_w_up, ffn1_w_down)
    ffn2_w = (ffn2_w_gate, ffn2_w_up, ffn2_w_down)
    for l in range(depth):
        x = _ffn(x.reshape(b * s, d), row(ffn1_norm[l]), *ffn1_w, l).reshape(b, s, d)

        wa, w_gates, wq, wk, wvt, ga, gb, gka, gkb, q_off, k_off, bounded = _prep_mixer_weights(
            w_in[l], w_uq[l], w_ukv[l], q_norm[l], k_norm[l])
        h, u, q, k, vt = _inproj(x, row(mix_norm[l]), wa, row(cq_norm[l]), row(ckv_norm[l]),
                                 wq, wk, wvt, ga, gb, gka, gkb, q_off, k_off, cos, sin)
        o = lax.cond(bounded, _attention_bounded, _attention, q, k, vt)
        x = _mix(x, h, u, o, w_gates, gate_bias[l].reshape(1, 2 * D_MODEL),
                 conv_w[l], row(conv_b[l]), row(conv_ln_g[l]), row(conv_ln_b[l]),
                 w_conv_out[l].astype(BF16), w_mla_out[l].astype(BF16),
                 w_out[l].astype(BF16))

        x = _ffn(x.reshape(b * s, d), row(ffn2_norm[l]), *ffn2_w, l).reshape(b, s, d)
    return x
```

```python
import math

import jax
import jax.numpy as jnp
from jax import lax
from jax.experimental import pallas as pl
from jax.experimental.pallas import tpu as pltpu

D_MODEL = 1024
D_FF = 2816
C_CONV = 512
CONV_WIDTH = 31
N_HEADS = 8
NOPE_DIM = 64
ROPE_DIM = 32
QK_DIM = NOPE_DIM + ROPE_DIM
V_DIM = 64
Q_LORA = 256
KV_LORA = 256
CHUNK = 64
ROPE_THETA = 10000.0
EPS = 1e-6

LANES = 128
SUBLANES = 8
MAX_FIXED_OFFSET = 50.0
HEAD_PAD = LANES
HALF_ROPE = ROPE_DIM // 2
VMEM_LIMIT = 56 * 1024 * 1024
FF_CHUNK = 256
N_FF_CHUNKS = D_FF // FF_CHUNK
TM_FFN = 1024
TM_PROJ = 1024
TM_MIX = 1024
TQ = 512
TK = 512
ATTN_Q_BLOCK = 2048
ATTN_HEADS = 4
HALO = 32
SHIFT_ROWS = TM_MIX + HALO - SUBLANES
CONV_ROWS = 64
F32 = jnp.float32
BF16 = jnp.bfloat16


def _const_spec(shape):
    zeros = (0,) * len(shape)
    return pl.BlockSpec(shape, lambda *_: zeros, pipeline_mode=pl.Buffered(1))


def _rms(x, g):
    ms = jnp.mean(x * x, axis=-1, keepdims=True)
    return x * lax.rsqrt(ms + EPS) * g


def _dot(a, b):
    return jnp.dot(a, b, preferred_element_type=F32)


def _rope_body(pos_ref, inv_ref, cos_ref, sin_ref):
    ang = pos_ref[...].astype(F32) * inv_ref[...]
    cos_ref[...] = jnp.cos(ang)
    sin_ref[...] = jnp.sin(ang)


def _rope_tables(positions):
    b, s = positions.shape
    t = b * s
    rows = t * HALF_ROPE // LANES
    inv_freq = ROPE_THETA ** (-jnp.arange(0, ROPE_DIM, 2, dtype=F32) / ROPE_DIM)
    pos_rep = jnp.broadcast_to(positions.reshape(t, 1), (t, HALF_ROPE)).reshape(rows, LANES)
    inv_rep = jnp.tile(inv_freq, LANES // HALF_ROPE).reshape(1, LANES)
    tr = min(rows, TM_PROJ)
    cos, sin = pl.pallas_call(
        _rope_body,
        name="rope_tables",
        grid=(rows // tr,),
        in_specs=[pl.BlockSpec((tr, LANES), lambda i: (i, 0)),
                  pl.BlockSpec((1, LANES), lambda i: (0, 0))],
        out_specs=[pl.BlockSpec((tr, LANES), lambda i: (i, 0))] * 2,
        out_shape=[jax.ShapeDtypeStruct((rows, LANES), F32)] * 2,
    )(pos_rep, inv_rep)
    return cos.reshape(b, s, HALF_ROPE), sin.reshape(b, s, HALF_ROPE)


def _ffn_value(x, g, wg_ref, wu_ref, wd_ref):
    xn = _rms(x, g).astype(BF16)
    acc = None
    for c in range(N_FF_CHUNKS):
        cols = slice(c * FF_CHUNK, (c + 1) * FF_CHUNK)
        gate = _dot(xn, wg_ref[0, :, cols].astype(BF16))
        up = _dot(xn, wu_ref[0, :, cols].astype(BF16))
        a = (gate * jax.nn.sigmoid(gate) * up).astype(BF16)
        d = _dot(a, wd_ref[0, cols, :].astype(BF16))
        acc = d if acc is None else acc + d
    return x + 0.5 * acc


def _ffn_body(x_ref, g_ref, wg_ref, wu_ref, wd_ref, o_ref):
    o_ref[...] = _ffn_value(x_ref[...], g_ref[...], wg_ref, wu_ref, wd_ref)


def _ffn(x2, g, w_gate, w_up, w_down, layer):
    t = x2.shape[0]
    tok = pl.BlockSpec((TM_FFN, D_MODEL), lambda i: (i, 0))
    layer_spec = lambda w: pl.BlockSpec((1,) + w.shape[1:], lambda i: (layer, 0, 0),
                                        pipeline_mode=pl.Buffered(1))
    return pl.pallas_call(
        _ffn_body,
        name="ffn",
        grid=(t // TM_FFN,),
        in_specs=[tok, _const_spec((1, D_MODEL)),
                  layer_spec(w_gate), layer_spec(w_up), layer_spec(w_down)],
        out_specs=tok,
        out_shape=jax.ShapeDtypeStruct((t, D_MODEL), F32),
        compiler_params=pltpu.CompilerParams(
            dimension_semantics=("arbitrary",), vmem_limit_bytes=VMEM_LIMIT),
    )(x2, g, w_gate, w_up, w_down)


def _inproj_body(x_ref, mixg_ref, wa_ref, cqg_ref, ckvg_ref, wq_ref, wk_ref, wvt_ref,
                 ga_ref, gb_ref, gka_ref, gkb_ref, qoff_ref, koff_ref, cos_ref, sin_ref,
                 h_ref, u_ref, q_ref, k_ref, vt_ref):
    h = _rms(x_ref[0], mixg_ref[...]).astype(BF16)
    h_ref[0] = h
    proj = _dot(h, wa_ref[...])
    o1 = 2 * C_CONV
    o2 = o1 + Q_LORA
    o3 = o2 + KV_LORA
    u_ref[0] = proj[:, :C_CONV] * jax.nn.sigmoid(proj[:, C_CONV:o1])
    cqn = _rms(proj[:, o1:o2], cqg_ref[...]).astype(BF16)
    ckvn = _rms(proj[:, o2:o3], ckvg_ref[...]).astype(BF16)
    kra = proj[:, o3:o3 + HEAD_PAD]
    krb = proj[:, o3 + HEAD_PAD:o3 + 2 * HEAD_PAD]
    qq = _dot(cqn, wq_ref[...])
    kv = _dot(ckvn, wk_ref[...])
    v_t = lax.dot_general(wvt_ref[...], ckvn, (((1,), (1,)), ((), ())),
                          preferred_element_type=F32).astype(BF16)
    for t in range(v_t.shape[1] // TK):
        vt_ref[0, t] = v_t[:, t * TK:(t + 1) * TK]
    cos = cos_ref[0]
    sin = sin_ref[0]
    tm = cos.shape[0]
    ca = jnp.concatenate([jnp.ones((tm, NOPE_DIM), F32), cos, cos,
                          jnp.zeros((tm, HEAD_PAD - QK_DIM), F32)], axis=1)
    sb = jnp.concatenate([jnp.zeros((tm, NOPE_DIM), F32), sin, sin,
                          jnp.zeros((tm, HEAD_PAD - QK_DIM), F32)], axis=1)
    q_cos = ga_ref[...] * ca
    q_sin = gb_ref[...] * sb
    gka = gka_ref[...]
    k_rot = kra * (gka * ca) + krb * (gkb_ref[...] * sb)
    ss_rope = jnp.sum(kra * kra, axis=-1, keepdims=True)
    hw = N_HEADS * HEAD_PAD
    for hh in range(N_HEADS):
        lo = hh * HEAD_PAD
        qa = qq[:, lo:lo + HEAD_PAD]
        qb = qq[:, hw + lo:hw + lo + HEAD_PAD]
        rq = lax.rsqrt(jnp.sum(qa * qa, axis=-1, keepdims=True) / QK_DIM + EPS)
        q_ref[0, hh] = (rq * (qa * q_cos + qb * q_sin) + qoff_ref[...]).astype(BF16)
        kn = kv[:, lo:lo + HEAD_PAD]
        ss = jnp.sum(kn * kn, axis=-1, keepdims=True) + ss_rope
        rk = lax.rsqrt(ss / QK_DIM + EPS)
        k_ref[0, hh] = (rk * (kn * gka + k_rot) + koff_ref[...]).astype(BF16)


def _inproj(x3, mixg, wa, cqg, ckvg, wq, wk, wvt, ga, gb, gka, gkb, q_off, k_off,
            cos, sin):
    b, s, _ = x3.shape
    tm = TM_PROJ
    hv = N_HEADS * V_DIM
    tok = lambda w: pl.BlockSpec((1, tm, w), lambda bi, i: (bi, i, 0))
    head = pl.BlockSpec((1, N_HEADS, tm, HEAD_PAD), lambda bi, i: (bi, 0, i, 0))
    vt_spec = pl.BlockSpec((1, tm // TK, hv, TK), lambda bi, i: (bi, i, 0, 0))
    return pl.pallas_call(
        _inproj_body,
        name="inproj",
        grid=(b, s // tm),
        in_specs=[tok(D_MODEL), _const_spec(mixg.shape), _const_spec(wa.shape),
                  _const_spec(cqg.shape), _const_spec(ckvg.shape),
                  _const_spec(wq.shape), _const_spec(wk.shape), _const_spec(wvt.shape),
                  _const_spec(ga.shape), _const_spec(gb.shape),
                  _const_spec(gka.shape), _const_spec(gkb.shape),
                  _const_spec(q_off.shape), _const_spec(k_off.shape),
                  tok(HALF_ROPE), tok(HALF_ROPE)],
        out_specs=[tok(D_MODEL), tok(C_CONV), head, head, vt_spec],
        out_shape=[jax.ShapeDtypeStruct((b, s, D_MODEL), BF16),
                   jax.ShapeDtypeStruct((b, s, C_CONV), F32),
                   jax.ShapeDtypeStruct((b, N_HEADS, s, HEAD_PAD), BF16),
                   jax.ShapeDtypeStruct((b, N_HEADS, s, HEAD_PAD), BF16),
                   jax.ShapeDtypeStruct((b, s // TK, hv, TK), BF16)],
        compiler_params=pltpu.CompilerParams(
            dimension_semantics=("arbitrary", "arbitrary"),
            vmem_limit_bytes=VMEM_LIMIT),
    )(x3, mixg, wa, cqg, ckvg, wq, wk, wvt, ga, gb, gka, gkb, q_off, k_off, cos, sin)


def _pad_lanes(w, lo, total=HEAD_PAD):
    pad = [(0, 0)] * (w.ndim - 1) + [(lo, total - lo - w.shape[-1])]
    return jnp.pad(w, pad)


def _swap_halves(w):
    return jnp.concatenate([w[..., HALF_ROPE:], w[..., :HALF_ROPE]], axis=-1)


def _prep_mixer_weights(w_in, w_uq, w_ukv, q_norm, k_norm):
    o1 = 2 * C_CONV
    o3 = o1 + Q_LORA + KV_LORA
    o4 = o3 + ROPE_DIM
    w_rope = w_in[:, o3:o4]
    wa = jnp.concatenate(
        [w_in[:, :o3], _pad_lanes(w_rope, NOPE_DIM),
         _pad_lanes(_swap_halves(w_rope), NOPE_DIM)], axis=-1).astype(BF16)
    w_gates = w_in[:, o4:].astype(BF16)

    wq3 = w_uq.reshape(Q_LORA, N_HEADS, QK_DIM)
    wq_a = _pad_lanes(wq3, 0).reshape(Q_LORA, N_HEADS * HEAD_PAD)
    wq_b = _pad_lanes(_swap_halves(wq3[..., NOPE_DIM:]), NOPE_DIM)
    wq_b = wq_b.reshape(Q_LORA, N_HEADS * HEAD_PAD)
    wq = jnp.concatenate([wq_a, wq_b], axis=-1).astype(BF16)

    wkv3 = w_ukv.reshape(KV_LORA, N_HEADS, NOPE_DIM + V_DIM)
    wk = _pad_lanes(wkv3[..., :NOPE_DIM], 0).reshape(KV_LORA, N_HEADS * HEAD_PAD)
    wk = wk.astype(BF16)
    wvt = wkv3[..., NOPE_DIM:].reshape(KV_LORA, N_HEADS * V_DIM).T.astype(BF16)

    q_scale = QK_DIM ** -0.5 * math.log2(math.e)
    sign = jnp.concatenate([-jnp.ones((HALF_ROPE,), F32), jnp.ones((HALF_ROPE,), F32)])
    ga = _pad_lanes(q_norm * q_scale, 0).reshape(1, HEAD_PAD)
    gb = _pad_lanes(_swap_halves(q_norm[NOPE_DIM:]) * sign * q_scale, NOPE_DIM)
    gb = gb.reshape(1, HEAD_PAD)
    gka = _pad_lanes(k_norm, 0).reshape(1, HEAD_PAD)
    gkb = _pad_lanes(_swap_halves(k_norm[NOPE_DIM:]) * sign, NOPE_DIM).reshape(1, HEAD_PAD)

    bound = QK_DIM * q_scale * jnp.max(jnp.abs(q_norm)) * jnp.max(jnp.abs(k_norm))
    bounded = bound <= MAX_FIXED_OFFSET
    lane = jnp.arange(HEAD_PAD) == QK_DIM
    q_off = jnp.where(lane & bounded, -bound, 0.0).astype(F32).reshape(1, HEAD_PAD)
    k_off = jnp.where(lane, 1.0, 0.0).astype(F32).reshape(1, HEAD_PAD)
    return wa, w_gates, wq, wk, wvt, ga, gb, gka, gkb, q_off, k_off, bounded


def _attn_body(q_ref, k_ref, vt_ref, o_ref, m_ref, l_ref, acc_ref):
    i = pl.program_id(2)
    key_chunk = lax.broadcasted_iota(jnp.int32, (TK, TQ), 0) // CHUNK
    qry_chunk = lax.broadcasted_iota(jnp.int32, (TK, TQ), 1) // CHUNK
    diag_mask = key_chunk <= qry_chunk

    def scores_t(hh, j):
        kb = k_ref[0, hh, pl.ds(pl.multiple_of(j * TK, TK), TK), :]
        return lax.dot_general(kb, q_ref[0, hh], (((1,), (1,)), ((), ())),
                               preferred_element_type=F32)

    def values_t(hh, j):
        return vt_ref[0, j, hh * V_DIM:(hh + 1) * V_DIM, :]

    for hh in range(2):
        s = jnp.where(diag_mask, scores_t(hh, i), -jnp.inf)
        m = jnp.max(s, axis=0, keepdims=True)
        p = jnp.exp2(s - m)
        m_ref[hh] = m
        l_ref[hh] = jnp.sum(p, axis=0, keepdims=True)
        acc_ref[hh] = _dot(values_t(hh, i), p.astype(BF16))

    def step(j, carry):
        for hh in range(2):
            s = scores_t(hh, j)
            m_old = m_ref[hh]
            m_new = jnp.maximum(m_old, jnp.max(s, axis=0, keepdims=True))
            alpha = jnp.exp2(m_old - m_new)
            p = jnp.exp2(s - m_new)
            l_ref[hh] = alpha * l_ref[hh] + jnp.sum(p, axis=0, keepdims=True)
            acc_ref[hh] = alpha * acc_ref[hh] + _dot(values_t(hh, j), p.astype(BF16))
            m_ref[hh] = m_new
        return carry

    lax.fori_loop(0, i, step, 0)

    o_t = jnp.concatenate([acc_ref[0] / l_ref[0], acc_ref[1] / l_ref[1]], axis=0)
    o_ref[0] = o_t.T.astype(BF16)


def _attn_bounded_body(q_ref, k_ref, vt_ref, o_ref):
    qb = ATTN_Q_BLOCK
    n_tiles = q_ref.shape[2] // qb

    n_sub = qb // TK

    def diag_mask(nq):
        key_chunk = lax.broadcasted_iota(jnp.int32, (TK, nq), 0) // CHUNK
        qry_chunk = lax.broadcasted_iota(jnp.int32, (TK, nq), 1) // CHUNK
        return key_chunk <= qry_chunk

    masks = {qb - a * TK: diag_mask(qb - a * TK) for a in range(n_sub)}

    for i in range(n_tiles):
        q_lo = i * qb
        sums = [[None] * n_sub for _ in range(ATTN_HEADS)]
        accs = [[None] * n_sub for _ in range(ATTN_HEADS)]
        for hh in range(ATTN_HEADS):
            for j in range((q_lo + qb) // TK):
                first = max(0, j - q_lo // TK)
                nq = qb - first * TK
                q = q_ref[0, hh, q_lo + first * TK:q_lo + qb, :]
                kb = k_ref[0, hh, j * TK:(j + 1) * TK, :]
                p = jnp.exp2(lax.dot_general(kb, q, (((1,), (1,)), ((), ())),
                                             preferred_element_type=F32))
                if j * TK >= q_lo:
                    p = jnp.where(masks[nq], p, 0.0)
                ps = jnp.sum(p.reshape(TK // SUBLANES, SUBLANES, nq), axis=0)
                pv = _dot(vt_ref[0, j, hh * V_DIM:(hh + 1) * V_DIM, :], p.astype(BF16))
                for a in range(first, n_sub):
                    cols = slice((a - first) * TK, (a - first + 1) * TK)
                    sums[hh][a] = ps[:, cols] if sums[hh][a] is None else sums[hh][a] + ps[:, cols]
                    accs[hh][a] = pv[:, cols] if accs[hh][a] is None else accs[hh][a] + pv[:, cols]
        for a in range(n_sub):
            o_t = jnp.concatenate(
                [accs[hh][a] / jnp.sum(sums[hh][a], axis=0, keepdims=True)
                 for hh in range(ATTN_HEADS)], axis=0)
            o_ref[0, q_lo + a * TK:q_lo + (a + 1) * TK, :] = o_t.T.astype(BF16)


def _attention_bounded(q, k, vt):
    b, _, s, _ = q.shape
    return pl.pallas_call(
        _attn_bounded_body,
        name="attn_bounded",
        grid=(b, N_HEADS // ATTN_HEADS),
        in_specs=[pl.BlockSpec((1, ATTN_HEADS, s, HEAD_PAD), lambda bi, p: (bi, p, 0, 0)),
                  pl.BlockSpec((1, ATTN_HEADS, s, HEAD_PAD), lambda bi, p: (bi, p, 0, 0)),
                  pl.BlockSpec((1, s // TK, ATTN_HEADS * V_DIM, TK),
                               lambda bi, p: (bi, 0, p, 0))],
        out_specs=pl.BlockSpec((1, s, ATTN_HEADS * V_DIM), lambda bi, p: (bi, 0, p)),
        out_shape=jax.ShapeDtypeStruct((b, s, N_HEADS * V_DIM), BF16),
        compiler_params=pltpu.CompilerParams(
            dimension_semantics=("arbitrary", "arbitrary"),
            vmem_limit_bytes=VMEM_LIMIT),
    )(q, k, vt)


def _attention(q, k, vt):
    b, _, s, _ = q.shape
    assert TQ == TK
    scratch = [pltpu.VMEM((2, 1, TQ), F32), pltpu.VMEM((2, 1, TQ), F32),
               pltpu.VMEM((2, V_DIM, TQ), F32)]
    return pl.pallas_call(
        _attn_body,
        name="attn",
        grid=(b, N_HEADS // 2, s // TQ),
        in_specs=[pl.BlockSpec((1, 2, TQ, HEAD_PAD), lambda bi, p, i: (bi, p, i, 0)),
                  pl.BlockSpec((1, 2, s, HEAD_PAD), lambda bi, p, i: (bi, p, 0, 0)),
                  pl.BlockSpec((1, s // TK, 2 * V_DIM, TK), lambda bi, p, i: (bi, 0, p, 0))],
        out_specs=pl.BlockSpec((1, TQ, 2 * V_DIM), lambda bi, p, i: (bi, i, p)),
        out_shape=jax.ShapeDtypeStruct((b, s, N_HEADS * V_DIM), BF16),
        scratch_shapes=scratch,
        compiler_params=pltpu.CompilerParams(
            dimension_semantics=("arbitrary", "arbitrary", "arbitrary"),
            vmem_limit_bytes=VMEM_LIMIT),
    )(q, k, vt)


def _conv_fill(u, halo, ubuf_ref, shift_ref):
    ubuf_ref[0:HALO, :] = halo
    ubuf_ref[HALO:HALO + TM_MIX, :] = u
    for bb in range(1, SUBLANES):
        shift_ref[bb - 1] = ubuf_ref[bb:bb + SHIFT_ROWS, :]


def _conv_branch(row0, nrows, cw_ref, cb_ref, lng_ref, lnb_ref, ubuf_ref, shift_ref, conv_ref):
    first = HALO - (CONV_WIDTH - 1)
    for cb in range(C_CONV // LANES):
        cols = slice(cb * LANES, (cb + 1) * LANES)
        for rb in range(row0 // CONV_ROWS, (row0 + nrows) // CONV_ROWS):
            acc = None
            for kk in range(CONV_WIDTH):
                aa, bb = divmod(first + kk, SUBLANES)
                r0 = aa * SUBLANES + rb * CONV_ROWS
                if bb == 0:
                    win = ubuf_ref[r0:r0 + CONV_ROWS, cols]
                else:
                    win = shift_ref[bb - 1, r0:r0 + CONV_ROWS, cols]
                term = cw_ref[kk:kk + 1, cols] * win
                acc = term if acc is None else acc + term
            conv_ref[rb * CONV_ROWS:(rb + 1) * CONV_ROWS, cols] = acc + cb_ref[:, cols]
    conv = conv_ref[row0:row0 + nrows, :]
    mu = jnp.mean(conv, axis=-1, keepdims=True)
    cc = conv - mu
    ln = cc * lax.rsqrt(jnp.mean(cc * cc, axis=-1, keepdims=True) + EPS)
    ln = ln * lng_ref[...] + lnb_ref[...]
    return (ln * jax.nn.sigmoid(ln)).astype(BF16)


def _mix_body(x_ref, h_ref, u_ref, halo_ref, o_ref, wg_ref, gbias_ref,
              cw_ref, cb_ref, lng_ref, lnb_ref, wco_ref, wmo_ref, wout_ref,
              out_ref, ubuf_ref, shift_ref, conv_ref):
    i = pl.program_id(1)
    halo = halo_ref[0]
    halo = jnp.where(i > 0, halo, jnp.zeros_like(halo))
    _conv_fill(u_ref[0], halo, ubuf_ref, shift_ref)
    act = _conv_branch(0, TM_MIX, cw_ref, cb_ref, lng_ref, lnb_ref,
                       ubuf_ref, shift_ref, conv_ref)
    gates = jax.nn.sigmoid(_dot(h_ref[0], wg_ref[...]) + gbias_ref[...])
    y_conv = _dot(act, wco_ref[...])
    y_mla = _dot(o_ref[0], wmo_ref[...])
    y = gates[:, :D_MODEL] * y_conv + gates[:, D_MODEL:] * y_mla
    out_ref[0] = x_ref[0] + _dot(y.astype(BF16), wout_ref[...])


def _mix(x3, h, u, o, w_gates, gbias, cw, cb, lng, lnb, wco, wmo, wout):
    b, s, _ = x3.shape
    tm = TM_MIX
    tok = lambda w: pl.BlockSpec((1, tm, w), lambda bi, i: (bi, i, 0))
    halo = pl.BlockSpec(
        (1, HALO, C_CONV), lambda bi, i: (bi, jnp.maximum(i * (tm // HALO) - 1, 0), 0))
    consts = [w_gates, gbias, cw, cb, lng, lnb, wco, wmo, wout]
    return pl.pallas_call(
        _mix_body,
        name="mix",
        grid=(b, s // tm),
        in_specs=[tok(D_MODEL), tok(D_MODEL), tok(C_CONV), halo, tok(N_HEADS * V_DIM)]
                 + [_const_spec(c.shape) for c in consts],
        out_specs=tok(D_MODEL),
        out_shape=jax.ShapeDtypeStruct((b, s, D_MODEL), F32),
        scratch_shapes=[pltpu.VMEM((HALO + tm, C_CONV), F32),
                        pltpu.VMEM((SUBLANES - 1, SHIFT_ROWS, C_CONV), F32),
                        pltpu.VMEM((tm, C_CONV), F32)],
        compiler_params=pltpu.CompilerParams(
            dimension_semantics=("arbitrary", "arbitrary"),
            vmem_limit_bytes=VMEM_LIMIT),
    )(x3, h, u, u, o, *consts)


def kernel(x, positions, ffn1_norm, ffn1_w_gate, ffn1_w_up, ffn1_w_down, mix_norm, w_in,
           gate_bias, conv_w, conv_b, conv_ln_g, conv_ln_b, w_conv_out, cq_norm, ckv_norm,
           w_uq, w_ukv, q_norm, k_norm, w_mla_out, w_out, ffn2_norm, ffn2_w_gate, ffn2_w_up,
           ffn2_w_down):
    b, s, d = x.shape
    depth = ffn1_norm.shape[0]
    assert d == D_MODEL and s % max(TM_PROJ, TM_MIX, TQ, ATTN_Q_BLOCK) == 0
    assert (b * s) % TM_FFN == 0 and ATTN_Q_BLOCK % TK == 0 and TM_PROJ % TK == 0
    row = lambda v: v.reshape(1, -1)
    cos, sin = _rope_tables(positions)
    ffn1_w = (ffn1_w_gate, ffn1_w_up, ffn1_w_down)
    ffn2_w = (ffn2_w_gate, ffn2_w_up, ffn2_w_down)
    for l in range(depth):
        x = _ffn(x.reshape(b * s, d), row(ffn1_norm[l]), *ffn1_w, l).reshape(b, s, d)

        wa, w_gates, wq, wk, wvt, ga, gb, gka, gkb, q_off, k_off, bounded = _prep_mixer_weights(
            w_in[l], w_uq[l], w_ukv[l], q_norm[l], k_norm[l])
        h, u, q, k, vt = _inproj(x, row(mix_norm[l]), wa, row(cq_norm[l]), row(ckv_norm[l]),
                                 wq, wk, wvt, ga, gb, gka, gkb, q_off, k_off, cos, sin)
        o = lax.cond(bounded, _attention_bounded, _attention, q, k, vt)
        x = _mix(x, h, u, o, w_gates, gate_bias[l].reshape(1, 2 * D_MODEL),
                 conv_w[l], row(conv_b[l]), row(conv_ln_g[l]), row(conv_ln_b[l]),
                 w_conv_out[l].astype(BF16), w_mla_out[l].astype(BF16),
                 w_out[l].astype(BF16))

        x = _ffn(x.reshape(b * s, d), row(ffn2_norm[l]), *ffn2_w, l).reshape(b, s, d)
    return x
```

```python
import math

import jax
import jax.numpy as jnp
from jax import lax
from jax.experimental import pallas as pl
from jax.experimental.pallas import tpu as pltpu

D_MODEL = 1024
D_FF = 2816
C_CONV = 512
CONV_WIDTH = 31
N_HEADS = 8
NOPE_DIM = 64
ROPE_DIM = 32
QK_DIM = NOPE_DIM + ROPE_DIM
V_DIM = 64
Q_LORA = 256
KV_LORA = 256
CHUNK = 64
ROPE_THETA = 10000.0
EPS = 1e-6

LANES = 128
SUBLANES = 8
MAX_FIXED_OFFSET = 50.0
HEAD_PAD = LANES
HALF_ROPE = ROPE_DIM // 2
VMEM_LIMIT = 56 * 1024 * 1024
FF_CHUNK = 256
N_FF_CHUNKS = D_FF // FF_CHUNK
TM_FFN = 1024
TM_PROJ = 1024
TM_MIX = 1024
TQ = 512
TK = 512
ATTN_Q_BLOCK = 2048
HALO = 32
SHIFT_ROWS = TM_MIX + HALO - SUBLANES
CONV_ROWS = 64
F32 = jnp.float32
BF16 = jnp.bfloat16


def _const_spec(shape):
    zeros = (0,) * len(shape)
    return pl.BlockSpec(shape, lambda *_: zeros, pipeline_mode=pl.Buffered(1))


def _rms(x, g):
    ms = jnp.mean(x * x, axis=-1, keepdims=True)
    return x * lax.rsqrt(ms + EPS) * g


def _dot(a, b):
    return jnp.dot(a, b, preferred_element_type=F32)


def _rope_body(pos_ref, inv_ref, cos_ref, sin_ref):
    ang = pos_ref[...].astype(F32) * inv_ref[...]
    cos_ref[...] = jnp.cos(ang)
    sin_ref[...] = jnp.sin(ang)


def _rope_tables(positions):
    b, s = positions.shape
    t = b * s
    rows = t * HALF_ROPE // LANES
    inv_freq = ROPE_THETA ** (-jnp.arange(0, ROPE_DIM, 2, dtype=F32) / ROPE_DIM)
    pos_rep = jnp.broadcast_to(positions.reshape(t, 1), (t, HALF_ROPE)).reshape(rows, LANES)
    inv_rep = jnp.tile(inv_freq, LANES // HALF_ROPE).reshape(1, LANES)
    tr = min(rows, TM_PROJ)
    cos, sin = pl.pallas_call(
        _rope_body,
        name="rope_tables",
        grid=(rows // tr,),
        in_specs=[pl.BlockSpec((tr, LANES), lambda i: (i, 0)),
                  pl.BlockSpec((1, LANES), lambda i: (0, 0))],
        out_specs=[pl.BlockSpec((tr, LANES), lambda i: (i, 0))] * 2,
        out_shape=[jax.ShapeDtypeStruct((rows, LANES), F32)] * 2,
    )(pos_rep, inv_rep)
    return cos.reshape(b, s, HALF_ROPE), sin.reshape(b, s, HALF_ROPE)


def _ffn_value(x, g, wg_ref, wu_ref, wd_ref):
    xn = _rms(x, g).astype(BF16)
    acc = None
    for c in range(N_FF_CHUNKS):
        cols = slice(c * FF_CHUNK, (c + 1) * FF_CHUNK)
        gate = _dot(xn, wg_ref[0, :, cols].astype(BF16))
        up = _dot(xn, wu_ref[0, :, cols].astype(BF16))
        a = (gate * jax.nn.sigmoid(gate) * up).astype(BF16)
        d = _dot(a, wd_ref[0, cols, :].astype(BF16))
        acc = d if acc is None else acc + d
    return x + 0.5 * acc


def _ffn_body(x_ref, g_ref, wg_ref, wu_ref, wd_ref, o_ref):
    o_ref[...] = _ffn_value(x_ref[...], g_ref[...], wg_ref, wu_ref, wd_ref)


def _ffn(x2, g, w_gate, w_up, w_down, layer):
    t = x2.shape[0]
    tok = pl.BlockSpec((TM_FFN, D_MODEL), lambda i: (i, 0))
    layer_spec = lambda w: pl.BlockSpec((1,) + w.shape[1:], lambda i: (layer, 0, 0),
                                        pipeline_mode=pl.Buffered(1))
    return pl.pallas_call(
        _ffn_body,
        name="ffn",
        grid=(t // TM_FFN,),
        in_specs=[tok, _const_spec((1, D_MODEL)),
                  layer_spec(w_gate), layer_spec(w_up), layer_spec(w_down)],
        out_specs=tok,
        out_shape=jax.ShapeDtypeStruct((t, D_MODEL), F32),
        compiler_params=pltpu.CompilerParams(
            dimension_semantics=("arbitrary",), vmem_limit_bytes=VMEM_LIMIT),
    )(x2, g, w_gate, w_up, w_down)


def _inproj_body(x_ref, mixg_ref, wa_ref, cqg_ref, ckvg_ref, wq_ref, wk_ref, wvt_ref,
                 ga_ref, gb_ref, gka_ref, gkb_ref, qoff_ref, koff_ref, cos_ref, sin_ref,
                 h_ref, u_ref, q_ref, k_ref, vt_ref):
    h = _rms(x_ref[0], mixg_ref[...]).astype(BF16)
    h_ref[0] = h
    proj = _dot(h, wa_ref[...])
    o1 = 2 * C_CONV
    o2 = o1 + Q_LORA
    o3 = o2 + KV_LORA
    u_ref[0] = proj[:, :C_CONV] * jax.nn.sigmoid(proj[:, C_CONV:o1])
    cqn = _rms(proj[:, o1:o2], cqg_ref[...]).astype(BF16)
    ckvn = _rms(proj[:, o2:o3], ckvg_ref[...]).astype(BF16)
    kra = proj[:, o3:o3 + HEAD_PAD]
    krb = proj[:, o3 + HEAD_PAD:o3 + 2 * HEAD_PAD]
    qq = _dot(cqn, wq_ref[...])
    kv = _dot(ckvn, wk_ref[...])
    v_t = lax.dot_general(wvt_ref[...], ckvn, (((1,), (1,)), ((), ())),
                          preferred_element_type=F32).astype(BF16)
    for t in range(v_t.shape[1] // TK):
        vt_ref[0, t] = v_t[:, t * TK:(t + 1) * TK]
    cos = cos_ref[0]
    sin = sin_ref[0]
    tm = cos.shape[0]
    ca = jnp.concatenate([jnp.ones((tm, NOPE_DIM), F32), cos, cos,
                          jnp.zeros((tm, HEAD_PAD - QK_DIM), F32)], axis=1)
    sb = jnp.concatenate([jnp.zeros((tm, NOPE_DIM), F32), sin, sin,
                          jnp.zeros((tm, HEAD_PAD - QK_DIM), F32)], axis=1)
    q_cos = ga_ref[...] * ca
    q_sin = gb_ref[...] * sb
    gka = gka_ref[...]
    k_rot = kra * (gka * ca) + krb * (gkb_ref[...] * sb)
    ss_rope = jnp.sum(kra * kra, axis=-1, keepdims=True)
    hw = N_HEADS * HEAD_PAD
    for hh in range(N_HEADS):
        lo = hh * HEAD_PAD
        qa = qq[:, lo:lo + HEAD_PAD]
        qb = qq[:, hw + lo:hw + lo + HEAD_PAD]
        rq = lax.rsqrt(jnp.sum(qa * qa, axis=-1, keepdims=True) / QK_DIM + EPS)
        q_ref[0, hh] = (rq * (qa * q_cos + qb * q_sin) + qoff_ref[...]).astype(BF16)
        kn = kv[:, lo:lo + HEAD_PAD]
        ss = jnp.sum(kn * kn, axis=-1, keepdims=True) + ss_rope
        rk = lax.rsqrt(ss / QK_DIM + EPS)
        k_ref[0, hh] = (rk * (kn * gka + k_rot) + koff_ref[...]).astype(BF16)


def _inproj(x3, mixg, wa, cqg, ckvg, wq, wk, wvt, ga, gb, gka, gkb, q_off, k_off,
            cos, sin):
    b, s, _ = x3.shape
    tm = TM_PROJ
    hv = N_HEADS * V_DIM
    tok = lambda w: pl.BlockSpec((1, tm, w), lambda bi, i: (bi, i, 0))
    head = pl.BlockSpec((1, N_HEADS, tm, HEAD_PAD), lambda bi, i: (bi, 0, i, 0))
    vt_spec = pl.BlockSpec((1, tm // TK, hv, TK), lambda bi, i: (bi, i, 0, 0))
    return pl.pallas_call(
        _inproj_body,
        name="inproj",
        grid=(b, s // tm),
        in_specs=[tok(D_MODEL), _const_spec(mixg.shape), _const_spec(wa.shape),
                  _const_spec(cqg.shape), _const_spec(ckvg.shape),
                  _const_spec(wq.shape), _const_spec(wk.shape), _const_spec(wvt.shape),
                  _const_spec(ga.shape), _const_spec(gb.shape),
                  _const_spec(gka.shape), _const_spec(gkb.shape),
                  _const_spec(q_off.shape), _const_spec(k_off.shape),
                  tok(HALF_ROPE), tok(HALF_ROPE)],
        out_specs=[tok(D_MODEL), tok(C_CONV), head, head, vt_spec],
        out_shape=[jax.ShapeDtypeStruct((b, s, D_MODEL), BF16),
                   jax.ShapeDtypeStruct((b, s, C_CONV), F32),
                   jax.ShapeDtypeStruct((b, N_HEADS, s, HEAD_PAD), BF16),
                   jax.ShapeDtypeStruct((b, N_HEADS, s, HEAD_PAD), BF16),
                   jax.ShapeDtypeStruct((b, s // TK, hv, TK), BF16)],
        compiler_params=pltpu.CompilerParams(
            dimension_semantics=("arbitrary", "arbitrary"),
            vmem_limit_bytes=VMEM_LIMIT),
    )(x3, mixg, wa, cqg, ckvg, wq, wk, wvt, ga, gb, gka, gkb, q_off, k_off, cos, sin)


def _pad_lanes(w, lo, total=HEAD_PAD):
    pad = [(0, 0)] * (w.ndim - 1) + [(lo, total - lo - w.shape[-1])]
    return jnp.pad(w, pad)


def _swap_halves(w):
    return jnp.concatenate([w[..., HALF_ROPE:], w[..., :HALF_ROPE]], axis=-1)


def _prep_mixer_weights(w_in, w_uq, w_ukv, q_norm, k_norm):
    o1 = 2 * C_CONV
    o3 = o1 + Q_LORA + KV_LORA
    o4 = o3 + ROPE_DIM
    w_rope = w_in[:, o3:o4]
    wa = jnp.concatenate(
        [w_in[:, :o3], _pad_lanes(w_rope, NOPE_DIM),
         _pad_lanes(_swap_halves(w_rope), NOPE_DIM)], axis=-1).astype(BF16)
    w_gates = w_in[:, o4:].astype(BF16)

    wq3 = w_uq.reshape(Q_LORA, N_HEADS, QK_DIM)
    wq_a = _pad_lanes(wq3, 0).reshape(Q_LORA, N_HEADS * HEAD_PAD)
    wq_b = _pad_lanes(_swap_halves(wq3[..., NOPE_DIM:]), NOPE_DIM)
    wq_b = wq_b.reshape(Q_LORA, N_HEADS * HEAD_PAD)
    wq = jnp.concatenate([wq_a, wq_b], axis=-1).astype(BF16)

    wkv3 = w_ukv.reshape(KV_LORA, N_HEADS, NOPE_DIM + V_DIM)
    wk = _pad_lanes(wkv3[..., :NOPE_DIM], 0).reshape(KV_LORA, N_HEADS * HEAD_PAD)
    wk = wk.astype(BF16)
    wvt = wkv3[..., NOPE_DIM:].reshape(KV_LORA, N_HEADS * V_DIM).T.astype(BF16)

    q_scale = QK_DIM ** -0.5 * math.log2(math.e)
    sign = jnp.concatenate([-jnp.ones((HALF_ROPE,), F32), jnp.ones((HALF_ROPE,), F32)])
    ga = _pad_lanes(q_norm * q_scale, 0).reshape(1, HEAD_PAD)
    gb = _pad_lanes(_swap_halves(q_norm[NOPE_DIM:]) * sign * q_scale, NOPE_DIM)
    gb = gb.reshape(1, HEAD_PAD)
    gka = _pad_lanes(k_norm, 0).reshape(1, HEAD_PAD)
    gkb = _pad_lanes(_swap_halves(k_norm[NOPE_DIM:]) * sign, NOPE_DIM).reshape(1, HEAD_PAD)

    bound = QK_DIM * q_scale * jnp.max(jnp.abs(q_norm)) * jnp.max(jnp.abs(k_norm))
    bounded = bound <= MAX_FIXED_OFFSET
    lane = jnp.arange(HEAD_PAD) == QK_DIM
    q_off = jnp.where(lane & bounded, -bound, 0.0).astype(F32).reshape(1, HEAD_PAD)
    k_off = jnp.where(lane, 1.0, 0.0).astype(F32).reshape(1, HEAD_PAD)
    return wa, w_gates, wq, wk, wvt, ga, gb, gka, gkb, q_off, k_off, bounded


def _attn_body(q_ref, k_ref, vt_ref, o_ref, m_ref, l_ref, acc_ref):
    i = pl.program_id(2)
    key_chunk = lax.broadcasted_iota(jnp.int32, (TK, TQ), 0) // CHUNK
    qry_chunk = lax.broadcasted_iota(jnp.int32, (TK, TQ), 1) // CHUNK
    diag_mask = key_chunk <= qry_chunk

    def scores_t(hh, j):
        kb = k_ref[0, hh, pl.ds(pl.multiple_of(j * TK, TK), TK), :]
        return lax.dot_general(kb, q_ref[0, hh], (((1,), (1,)), ((), ())),
                               preferred_element_type=F32)

    def values_t(hh, j):
        return vt_ref[0, j, hh * V_DIM:(hh + 1) * V_DIM, :]

    for hh in range(2):
        s = jnp.where(diag_mask, scores_t(hh, i), -jnp.inf)
        m = jnp.max(s, axis=0, keepdims=True)
        p = jnp.exp2(s - m)
        m_ref[hh] = m
        l_ref[hh] = jnp.sum(p, axis=0, keepdims=True)
        acc_ref[hh] = _dot(values_t(hh, i), p.astype(BF16))

    def step(j, carry):
        for hh in range(2):
            s = scores_t(hh, j)
            m_old = m_ref[hh]
            m_new = jnp.maximum(m_old, jnp.max(s, axis=0, keepdims=True))
            alpha = jnp.exp2(m_old - m_new)
            p = jnp.exp2(s - m_new)
            l_ref[hh] = alpha * l_ref[hh] + jnp.sum(p, axis=0, keepdims=True)
            acc_ref[hh] = alpha * acc_ref[hh] + _dot(values_t(hh, j), p.astype(BF16))
            m_ref[hh] = m_new
        return carry

    lax.fori_loop(0, i, step, 0)

    o_t = jnp.concatenate([acc_ref[0] / l_ref[0], acc_ref[1] / l_ref[1]], axis=0)
    o_ref[0] = o_t.T.astype(BF16)


def _attn_bounded_body(q_ref, k_ref, vt_ref, o_ref):
    qb = ATTN_Q_BLOCK
    n_tiles = q_ref.shape[2] // qb

    n_sub = qb // TK

    def diag_mask(nq):
        key_chunk = lax.broadcasted_iota(jnp.int32, (TK, nq), 0) // CHUNK
        qry_chunk = lax.broadcasted_iota(jnp.int32, (TK, nq), 1) // CHUNK
        return key_chunk <= qry_chunk

    masks = {qb - a * TK: diag_mask(qb - a * TK) for a in range(n_sub)}

    for i in range(n_tiles):
        q_lo = i * qb
        sums = [[None] * n_sub for _ in range(2)]
        accs = [[None] * n_sub for _ in range(2)]
        for hh in range(2):
            for j in range((q_lo + qb) // TK):
                first = max(0, j - q_lo // TK)
                nq = qb - first * TK
                q = q_ref[0, hh, q_lo + first * TK:q_lo + qb, :]
                kb = k_ref[0, hh, j * TK:(j + 1) * TK, :]
                p = jnp.exp2(lax.dot_general(kb, q, (((1,), (1,)), ((), ())),
                                             preferred_element_type=F32))
                if j * TK >= q_lo:
                    p = jnp.where(masks[nq], p, 0.0)
                ps = jnp.sum(p.reshape(TK // SUBLANES, SUBLANES, nq), axis=0)
                pv = _dot(vt_ref[0, j, hh * V_DIM:(hh + 1) * V_DIM, :], p.astype(BF16))
                for a in range(first, n_sub):
                    cols = slice((a - first) * TK, (a - first + 1) * TK)
                    sums[hh][a] = ps[:, cols] if sums[hh][a] is None else sums[hh][a] + ps[:, cols]
                    accs[hh][a] = pv[:, cols] if accs[hh][a] is None else accs[hh][a] + pv[:, cols]
        for a in range(n_sub):
            o_t = jnp.concatenate(
                [accs[hh][a] / jnp.sum(sums[hh][a], axis=0, keepdims=True) for hh in range(2)],
                axis=0)
            o_ref[0, q_lo + a * TK:q_lo + (a + 1) * TK, :] = o_t.T.astype(BF16)


def _attention_bounded(q, k, vt):
    b, _, s, _ = q.shape
    return pl.pallas_call(
        _attn_bounded_body,
        name="attn_bounded",
        grid=(b, N_HEADS // 2),
        in_specs=[pl.BlockSpec((1, 2, s, HEAD_PAD), lambda bi, p: (bi, p, 0, 0)),
                  pl.BlockSpec((1, 2, s, HEAD_PAD), lambda bi, p: (bi, p, 0, 0)),
                  pl.BlockSpec((1, s // TK, 2 * V_DIM, TK), lambda bi, p: (bi, 0, p, 0))],
        out_specs=pl.BlockSpec((1, s, 2 * V_DIM), lambda bi, p: (bi, 0, p)),
        out_shape=jax.ShapeDtypeStruct((b, s, N_HEADS * V_DIM), BF16),
        compiler_params=pltpu.CompilerParams(
            dimension_semantics=("arbitrary", "arbitrary"),
            vmem_limit_bytes=VMEM_LIMIT),
    )(q, k, vt)


def _attention(q, k, vt):
    b, _, s, _ = q.shape
    assert TQ == TK
    scratch = [pltpu.VMEM((2, 1, TQ), F32), pltpu.VMEM((2, 1, TQ), F32),
               pltpu.VMEM((2, V_DIM, TQ), F32)]
    return pl.pallas_call(
        _attn_body,
        name="attn",
        grid=(b, N_HEADS // 2, s // TQ),
        in_specs=[pl.BlockSpec((1, 2, TQ, HEAD_PAD), lambda bi, p, i: (bi, p, i, 0)),
                  pl.BlockSpec((1, 2, s, HEAD_PAD), lambda bi, p, i: (bi, p, 0, 0)),
                  pl.BlockSpec((1, s // TK, 2 * V_DIM, TK), lambda bi, p, i: (bi, 0, p, 0))],
        out_specs=pl.BlockSpec((1, TQ, 2 * V_DIM), lambda bi, p, i: (bi, i, p)),
        out_shape=jax.ShapeDtypeStruct((b, s, N_HEADS * V_DIM), BF16),
        scratch_shapes=scratch,
        compiler_params=pltpu.CompilerParams(
            dimension_semantics=("arbitrary", "arbitrary", "arbitrary"),
            vmem_limit_bytes=VMEM_LIMIT),
    )(q, k, vt)


def _conv_fill(u, halo, ubuf_ref, shift_ref):
    ubuf_ref[0:HALO, :] = halo
    ubuf_ref[HALO:HALO + TM_MIX, :] = u
    for bb in range(1, SUBLANES):
        shift_ref[bb - 1] = ubuf_ref[bb:bb + SHIFT_ROWS, :]


def _conv_branch(row0, nrows, cw_ref, cb_ref, lng_ref, lnb_ref, ubuf_ref, shift_ref, conv_ref):
    first = HALO - (CONV_WIDTH - 1)
    for cb in range(C_CONV // LANES):
        cols = slice(cb * LANES, (cb + 1) * LANES)
        for rb in range(row0 // CONV_ROWS, (row0 + nrows) // CONV_ROWS):
            acc = None
            for kk in range(CONV_WIDTH):
                aa, bb = divmod(first + kk, SUBLANES)
                r0 = aa * SUBLANES + rb * CONV_ROWS
                if bb == 0:
                    win = ubuf_ref[r0:r0 + CONV_ROWS, cols]
                else:
                    win = shift_ref[bb - 1, r0:r0 + CONV_ROWS, cols]
                term = cw_ref[kk:kk + 1, cols] * win
                acc = term if acc is None else acc + term
            conv_ref[rb * CONV_ROWS:(rb + 1) * CONV_ROWS, cols] = acc + cb_ref[:, cols]
    conv = conv_ref[row0:row0 + nrows, :]
    mu = jnp.mean(conv, axis=-1, keepdims=True)
    cc = conv - mu
    ln = cc * lax.rsqrt(jnp.mean(cc * cc, axis=-1, keepdims=True) + EPS)
    ln = ln * lng_ref[...] + lnb_ref[...]
    return (ln * jax.nn.sigmoid(ln)).astype(BF16)


def _conv_body(u_ref, halo_ref, cw_ref, cb_ref, lng_ref, lnb_ref, act_ref,
               ubuf_ref, shift_ref, conv_ref):
    halo = halo_ref[0]
    halo = jnp.where(pl.program_id(1) > 0, halo, jnp.zeros_like(halo))
    _conv_fill(u_ref[0], halo, ubuf_ref, shift_ref)
    act_ref[0] = _conv_branch(0, TM_MIX, cw_ref, cb_ref, lng_ref, lnb_ref,
                              ubuf_ref, shift_ref, conv_ref)


def _conv_call(u, cw, cb, lng, lnb):
    b, s, _ = u.shape
    tm = TM_MIX
    tok = pl.BlockSpec((1, tm, C_CONV), lambda bi, i: (bi, i, 0))
    halo = pl.BlockSpec(
        (1, HALO, C_CONV), lambda bi, i: (bi, jnp.maximum(i * (tm // HALO) - 1, 0), 0))
    consts = [cw, cb, lng, lnb]
    return pl.pallas_call(
        _conv_body,
        name="conv_branch",
        grid=(b, s // tm),
        in_specs=[tok, halo] + [_const_spec(c.shape) for c in consts],
        out_specs=tok,
        out_shape=jax.ShapeDtypeStruct((b, s, C_CONV), BF16),
        scratch_shapes=[pltpu.VMEM((HALO + tm, C_CONV), F32),
                        pltpu.VMEM((SUBLANES - 1, SHIFT_ROWS, C_CONV), F32),
                        pltpu.VMEM((tm, C_CONV), F32)],
        compiler_params=pltpu.CompilerParams(
            dimension_semantics=("arbitrary", "arbitrary"),
            vmem_limit_bytes=VMEM_LIMIT),
    )(u, u, *consts)


def _mix_body(x_ref, h_ref, act_ref, o_ref, wg_ref, gbias_ref, wco_ref, wmo_ref, wout_ref,
              out_ref):
    gates = jax.nn.sigmoid(_dot(h_ref[0], wg_ref[...]) + gbias_ref[...])
    y_conv = _dot(act_ref[0], wco_ref[...])
    y_mla = _dot(o_ref[0], wmo_ref[...])
    y = gates[:, :D_MODEL] * y_conv + gates[:, D_MODEL:] * y_mla
    out_ref[0] = x_ref[0] + _dot(y.astype(BF16), wout_ref[...])


def _mix(x3, h, act, o, w_gates, gbias, wco, wmo, wout):
    b, s, _ = x3.shape
    tm = TM_MIX
    tok = lambda w: pl.BlockSpec((1, tm, w), lambda bi, i: (bi, i, 0))
    consts = [w_gates, gbias, wco, wmo, wout]
    return pl.pallas_call(
        _mix_body,
        name="mix",
        grid=(b, s // tm),
        in_specs=[tok(D_MODEL), tok(D_MODEL), tok(C_CONV), tok(N_HEADS * V_DIM)]
                 + [_const_spec(c.shape) for c in consts],
        out_specs=tok(D_MODEL),
        out_shape=jax.ShapeDtypeStruct((b, s, D_MODEL), F32),
        compiler_params=pltpu.CompilerParams(
            dimension_semantics=("arbitrary", "arbitrary"),
            vmem_limit_bytes=VMEM_LIMIT),
    )(x3, h, act, o, *consts)


def kernel(x, positions, ffn1_norm, ffn1_w_gate, ffn1_w_up, ffn1_w_down, mix_norm, w_in,
           gate_bias, conv_w, conv_b, conv_ln_g, conv_ln_b, w_conv_out, cq_norm, ckv_norm,
           w_uq, w_ukv, q_norm, k_norm, w_mla_out, w_out, ffn2_norm, ffn2_w_gate, ffn2_w_up,
           ffn2_w_down):
    b, s, d = x.shape
    depth = ffn1_norm.shape[0]
    assert d == D_MODEL and s % max(TM_PROJ, TM_MIX, TQ, ATTN_Q_BLOCK) == 0
    assert (b * s) % TM_FFN == 0 and ATTN_Q_BLOCK % TK == 0 and TM_PROJ % TK == 0
    row = lambda v: v.reshape(1, -1)
    cos, sin = _rope_tables(positions)
    ffn1_w = (ffn1_w_gate, ffn1_w_up, ffn1_w_down)
    ffn2_w = (ffn2_w_gate, ffn2_w_up, ffn2_w_down)
    for l in range(depth):
        x = _ffn(x.reshape(b * s, d), row(ffn1_norm[l]), *ffn1_w, l).reshape(b, s, d)

        wa, w_gates, wq, wk, wvt, ga, gb, gka, gkb, q_off, k_off, bounded = _prep_mixer_weights(
            w_in[l], w_uq[l], w_ukv[l], q_norm[l], k_norm[l])
        h, u, q, k, vt = _inproj(x, row(mix_norm[l]), wa, row(cq_norm[l]), row(ckv_norm[l]),
                                 wq, wk, wvt, ga, gb, gka, gkb, q_off, k_off, cos, sin)
        o = lax.cond(bounded, _attention_bounded, _attention, q, k, vt)
        act = _conv_call(u, conv_w[l], row(conv_b[l]), row(conv_ln_g[l]), row(conv_ln_b[l]))
        x = _mix(x, h, act, o, w_gates, gate_bias[l].reshape(1, 2 * D_MODEL),
                 w_conv_out[l].astype(BF16), w_mla_out[l].astype(BF16),
                 w_out[l].astype(BF16))

        x = _ffn(x.reshape(b * s, d), row(ffn2_norm[l]), *ffn2_w, l).reshape(b, s, d)
    return x
```
